```python
import math
import jax, jax.numpy as jnp
from jax import lax
import numpy as np

D_MODEL = 1024
BATCH = 8
SEQ = 2048
DEPTH = 2

D_FF = 2816
HALF_STEP = 0.5
N_MOD = 9
EPS = 1e-6

SB_HEADS = 4
SB_DIM = 64
SB_BLOCK = 128
SB_W = SB_HEADS * SB_DIM

GDN_HEADS = 4
GDN_DK = 128
GDN_DV = 128
GDN_CONV = 4
GDN_CHUNK = 64
GDN_WK = GDN_HEADS * GDN_DK
GDN_WV = GDN_HEADS * GDN_DV

HG_HEADS = 4
HG_DK = 64
HG_DV = 64
HG_CHUNK = 64
HG_WK = HG_HEADS * HG_DK
HG_WV = HG_HEADS * HG_DV

D_MIX = SB_W + GDN_WV + HG_WV
IN_SIZES = (SB_W, SB_W, SB_W, GDN_WK, GDN_WK, GDN_WV, GDN_WV, GDN_HEADS, GDN_HEADS,
            HG_WK, HG_WK, HG_WV, HG_WV)
D_IN = sum(IN_SIZES)
SPLITS = tuple(int(s) for s in np.cumsum(IN_SIZES)[:-1])

kernel_name = 'hybrid_sb_gdn_hgrn2_macaron_adaln'


def _rms_norm(x, gain):
    x32 = x.astype(jnp.float32)
    y = x32 * lax.rsqrt(jnp.mean(x32 * x32, axis=-1, keepdims=True) + EPS)
    return (y * gain.astype(jnp.float32)).astype(x.dtype)


def _l2_normalize(x):
    x32 = x.astype(jnp.float32)
    return x32 * lax.rsqrt(jnp.sum(x32 * x32, axis=-1, keepdims=True) + EPS)


def _modulate(h, shift, scale):
    return h * (1.0 + scale[:, None, :]) + shift[:, None, :]


def _swiglu(h, w_in, w_out):
    gate, up = jnp.split(h @ w_in, 2, axis=-1)
    return (jax.nn.silu(gate) * up) @ w_out


def _causal_depthwise_conv(x, w):
    return lax.conv_general_dilated(
        x, w[:, None, :].astype(x.dtype), window_strides=(1,),
        padding=((GDN_CONV - 1, 0),), dimension_numbers=('NWC', 'WIO', 'NWC'),
        feature_group_count=x.shape[-1])


def _to_chunks(x, chunk):
    b, t, h = x.shape[:3]
    x = x.reshape((b, t // chunk, chunk, h) + x.shape[3:])
    x = jnp.moveaxis(x, 3, 2)
    return jnp.moveaxis(x, 1, 0)


def _from_chunks(o):
    n, b, h, c, e = o.shape
    return o.transpose(1, 0, 3, 2, 4).reshape(b, n * c, h, e)


def _stick_breaking_attention(q, k, v):
    seq, d = q.shape[1], q.shape[3]
    q = q.transpose(0, 2, 1, 3).astype(jnp.float32)
    k = k.transpose(0, 2, 1, 3).astype(jnp.float32)
    v = v.transpose(0, 2, 1, 3).astype(jnp.float32)
    scale = d ** -0.5
    outs = []
    for blk in range(seq // SB_BLOCK):
        start = blk * SB_BLOCK
        end = start + SB_BLOCK
        qb = q[:, :, start:end]
        kb = k[:, :, :end]
        vb = v[:, :, :end]
        z = jnp.einsum('bhtd,bhsd->bhts', qb, kb) * scale
        t_idx = start + jnp.arange(SB_BLOCK)[:, None]
        s_idx = jnp.arange(end)[None, :]
        causal = s_idx < t_idx
        log_keep = jnp.where(causal, -jax.nn.softplus(z), 0.0)
        between = lax.cumsum(log_keep, axis=3, reverse=True) - log_keep
        log_w = jnp.where(causal, jax.nn.log_sigmoid(z) + between, -jnp.inf)
        outs.append(jnp.einsum('bhts,bhsd->bhtd', jnp.exp(log_w), vb))
    o = jnp.concatenate(outs, axis=2)
    return o.transpose(0, 2, 1, 3)


def _gated_delta_rule(q, k, v, log_a, beta):
    c = GDN_CHUNK
    dk, dv = q.shape[-1], v.shape[-1]
    q, k, v = (_to_chunks(t, c) for t in (q, k, v))
    log_a = _to_chunks(log_a, c)
    beta = _to_chunks(beta, c)
    g = jnp.cumsum(log_a, axis=-1)
    incl = jnp.tril(jnp.ones((c, c), dtype=bool))
    strict = jnp.tril(jnp.ones((c, c), dtype=bool), k=-1)
    decay = jnp.exp(jnp.where(incl, g[..., :, None] - g[..., None, :], -jnp.inf))
    k_beta = k * beta[..., None]
    a_kk = jnp.where(strict, jnp.einsum('nbhcd,nbhsd->nbhcs', k_beta, k) * decay, 0.0)
    eye = jnp.eye(c, dtype=jnp.float32)
    rhs = jnp.concatenate([v * beta[..., None], k_beta * jnp.exp(g)[..., None]], axis=-1)
    sol = lax.linalg.triangular_solve(eye + a_kk, rhs, left_side=True, lower=True,
                                      unit_diagonal=True)
    u, w = sol[..., :dv], sol[..., dv:]
    a_qk = jnp.einsum('nbhcd,nbhsd->nbhcs', q, k) * decay
    q_dec = q * jnp.exp(g)[..., None]
    k_dec = k * jnp.exp(g[..., -1:] - g)[..., None]
    chunk_decay = jnp.exp(g[..., -1])

    def step(state, xs):
        u_c, w_c, a_c, qd_c, kd_c, cd_c = xs
        v_new = u_c - jnp.einsum('bhcd,bhde->bhce', w_c, state)
        o = jnp.einsum('bhcd,bhde->bhce', qd_c, state) + jnp.einsum('bhcs,bhse->bhce', a_c, v_new)
        state = state * cd_c[..., None, None] + jnp.einsum('bhcd,bhce->bhde', kd_c, v_new)
        return state, o

    state0 = jnp.zeros(q.shape[1:3] + (dk, dv), jnp.float32)
    _, o = lax.scan(step, state0, (u, w, a_qk, q_dec, k_dec, chunk_decay))
    return _from_chunks(o)


def _hgrn2_recurrence(q, k, v, log_f):
    c = HG_CHUNK
    dk, dv = q.shape[-1], v.shape[-1]
    q, k, v, log_f = (_to_chunks(t, c) for t in (q, k, v, log_f))
    b = jnp.cumsum(log_f, axis=-2)
    incl = jnp.tril(jnp.ones((c, c), dtype=bool))[:, :, None]

    def step(state, xs):
        q_c, k_c, v_c, b_c = xs
        pair = jnp.exp(jnp.where(incl, b_c[..., :, None, :] - b_c[..., None, :, :], -jnp.inf))
        attn = jnp.einsum('bhtd,bhtsd,bhsd->bhts', q_c, pair, k_c)
        b_last = b_c[..., -1:, :]
        o = (jnp.einsum('bhtd,bhde->bhte', q_c * jnp.exp(b_c), state)
             + jnp.einsum('bhts,bhse->bhte', attn, v_c))
        state = (jnp.exp(b_last[..., 0, :])[..., None] * state
                 + jnp.einsum('bhsd,bhse->bhde', k_c * jnp.exp(b_last - b_c), v_c))
        return state, o

    state0 = jnp.zeros(q.shape[1:3] + (dk, dv), jnp.float32)
    _, o = lax.scan(step, state0, (q, k, v, b))
    return _from_chunks(o)


def _hybrid_mixer(h, w_in, conv_w, A_log, dt_bias, sb_gain, gdn_gain, hg_gain, lower_bound, w_out):
    bsz, seq, _ = h.shape
    f32 = jnp.float32
    proj = h @ w_in
    (sb_q, sb_k, sb_v, gd_q, gd_k, gd_v, gd_z, gd_b, gd_a,
     hg_q, hg_f, hg_i, hg_g) = jnp.split(proj, SPLITS, axis=-1)

    def heads(t, n):
        return t.reshape(bsz, seq, n, -1)

    o_sb = _stick_breaking_attention(heads(sb_q, SB_HEADS), heads(sb_k, SB_HEADS),
                                     heads(sb_v, SB_HEADS))
    o_sb = _rms_norm(o_sb, sb_gain).reshape(bsz, seq, SB_W)

    qkv = jax.nn.silu(_causal_depthwise_conv(jnp.concatenate([gd_q, gd_k, gd_v], axis=-1), conv_w))
    gq, gk, gv = jnp.split(qkv, [GDN_WK, 2 * GDN_WK], axis=-1)
    gq = _l2_normalize(heads(gq, GDN_HEADS)) * (GDN_DK ** -0.5)
    gk = _l2_normalize(heads(gk, GDN_HEADS))
    gv = heads(gv, GDN_HEADS).astype(f32)
    beta = jax.nn.sigmoid(gd_b.astype(f32))
    log_a = -jnp.exp(A_log.astype(f32)) * jax.nn.softplus(gd_a.astype(f32) + dt_bias.astype(f32))
    o_gd = _gated_delta_rule(gq, gk, gv, log_a, beta)
    o_gd = (_rms_norm(o_gd, gdn_gain) * jax.nn.silu(heads(gd_z, GDN_HEADS).astype(f32)))
    o_gd = o_gd.reshape(bsz, seq, GDN_WV)

    lb = lower_bound.astype(f32)
    f_pre = hg_f.astype(f32)
    log_f = jnp.logaddexp(jnp.log(lb), jnp.log1p(-lb) + jax.nn.log_sigmoid(f_pre))
    k_in = (1.0 - lb) * jax.nn.sigmoid(-f_pre)
    q_c = jax.nn.silu(hg_q.astype(f32))
    o_hg = _hgrn2_recurrence(heads(q_c, HG_HEADS), heads(k_in, HG_HEADS),
                             heads(hg_i.astype(f32), HG_HEADS), heads(log_f, HG_HEADS))
    o_hg = (_rms_norm(o_hg, hg_gain) * jax.nn.silu(heads(hg_g, HG_HEADS).astype(f32)))
    o_hg = o_hg.reshape(bsz, seq, HG_WV)

    o = jnp.concatenate([o_sb.astype(f32), o_gd, o_hg], axis=-1).astype(h.dtype)
    return o @ w_out


def setup_inputs(seed: int = 0) -> dict:
    key = jax.random.key(seed)
    ks = jax.random.split(key, 24)

    def nrm(k, shape, scale):
        return jax.random.normal(k, shape, jnp.float32) * scale

    def gain(k, shape):
        return 1.0 + nrm(k, shape, 0.02)

    dt = jnp.exp(jax.random.uniform(ks[10], (DEPTH, GDN_HEADS), jnp.float32,
                                    minval=math.log(1e-3), maxval=math.log(1e-1)))
    return {
        'x': nrm(ks[0], (BATCH, SEQ, D_MODEL), 1.0),
        'c': nrm(ks[1], (BATCH, D_MODEL), 1.0),
        'ffn1_norm': gain(ks[2], (DEPTH, D_MODEL)),
        'ffn1_w_in': nrm(ks[3], (DEPTH, D_MODEL, 2 * D_FF), D_MODEL ** -0.5),
        'ffn1_w_out': nrm(ks[4], (DEPTH, D_FF, D_MODEL), D_FF ** -0.5),
        'mix_norm': gain(ks[5], (DEPTH, D_MODEL)),
        'mix_w_in': nrm(ks[6], (DEPTH, D_MODEL, D_IN), D_MODEL ** -0.5),
        'gdn_conv_w': nrm(ks[7], (DEPTH, GDN_CONV, GDN_WK * 2 + GDN_WV), GDN_CONV ** -0.5),
        'gdn_A_log': jnp.log(jax.random.uniform(ks[8], (DEPTH, GDN_HEADS), jnp.float32,
                                                minval=1.0, maxval=16.0)),
        'gdn_dt_bias': dt + jnp.log(-jnp.expm1(-dt)),
        'sb_out_norm': gain(ks[11], (DEPTH, SB_DIM)),
        'gdn_out_norm': gain(ks[12], (DEPTH, GDN_DV)),
        'hg_out_norm': gain(ks[13], (DEPTH, HG_DV)),
        'hg_lb_logits': nrm(ks[14], (DEPTH, HG_WK), 0.1),
        'mix_w_out': nrm(ks[15], (DEPTH, D_MIX, D_MODEL), D_MIX ** -0.5),
        'ffn2_norm': gain(ks[16], (DEPTH, D_MODEL)),
        'ffn2_w_in': nrm(ks[17], (DEPTH, D_MODEL, 2 * D_FF), D_MODEL ** -0.5),
        'ffn2_w_out': nrm(ks[18], (DEPTH, D_FF, D_MODEL), D_FF ** -0.5),
        'ada_w': nrm(ks[19], (DEPTH, D_MODEL, N_MOD * D_MODEL), 0.5 * D_MODEL ** -0.5),
        'ada_b': nrm(ks[20], (DEPTH, N_MOD * D_MODEL), 0.01),
        'final_norm': gain(ks[21], (D_MODEL,)),
    }


def reference(x, c, ffn1_norm, ffn1_w_in, ffn1_w_out, mix_norm, mix_w_in, gdn_conv_w,
              gdn_A_log, gdn_dt_bias, sb_out_norm, gdn_out_norm, hg_out_norm, hg_lb_logits,
              mix_w_out, ffn2_norm, ffn2_w_in, ffn2_w_out, ada_w, ada_b, final_norm):
    lb_cum = jnp.cumsum(jax.nn.softmax(hg_lb_logits.astype(jnp.float32), axis=0), axis=0)
    lower_bounds = lb_cum - lb_cum[0:1]
    c_act = jax.nn.silu(c)
    bsz = x.shape[0]
    for layer in range(DEPTH):
        mod = (c_act @ ada_w[layer] + ada_b[layer]).reshape(bsz, N_MOD, D_MODEL)
        m = [mod[:, i] for i in range(N_MOD)]
        h = _modulate(_rms_norm(x, ffn1_norm[layer]), m[0], m[1])
        x = x + HALF_STEP * m[2][:, None, :] * _swiglu(h, ffn1_w_in[layer], ffn1_w_out[layer])
        h = _modulate(_rms_norm(x, mix_norm[layer]), m[3], m[4])
        y = _hybrid_mixer(h, mix_w_in[layer], gdn_conv_w[layer], gdn_A_log[layer],
                          gdn_dt_bias[layer], sb_out_norm[layer], gdn_out_norm[layer],
                          hg_out_norm[layer], lower_bounds[layer], mix_w_out[layer])
        x = x + m[5][:, None, :] * y
        h = _modulate(_rms_norm(x, ffn2_norm[layer]), m[6], m[7])
        x = x + HALF_STEP * m[8][:, None, :] * _swiglu(h, ffn2_w_in[layer], ffn2_w_out[layer])
    return _rms_norm(x, final_norm)
```

```python
import functools

import jax
import jax.numpy as jnp
from jax import lax
from jax.experimental import pallas as pl
from jax.experimental.pallas import tpu as pltpu

F32 = jnp.float32
BF16 = jnp.bfloat16

EPS = 1e-6
HALF_STEP = 0.5
N_MOD = 9

SB_HEADS, SB_DIM = 4, 64
GDN_HEADS, GDN_DK, GDN_DV, GDN_CONV, GDN_CHUNK = 4, 128, 128, 4, 64
HG_HEADS, HG_DK, HG_DV, HG_CHUNK = 4, 64, 64, 64

LANES = 128
SUBLANES = 8
VMEM_LIMIT = 56 * 1024 * 1024

_NT = (((1,), (1,)), ((), ()))
_TN = (((0,), (0,)), ((), ()))


def _dot(a, b):
    return jnp.dot(a, b, preferred_element_type=F32)


def _dot_nt(a, b):
    return lax.dot_general(a, b, _NT, preferred_element_type=F32)


def _dot_tn(a, b):
    return lax.dot_general(a, b, _TN, preferred_element_type=F32)


def _split(x, parts):
    out = []
    r = x
    for _ in range(parts):
        p = r.astype(BF16)
        out.append(p)
        r = r - p.astype(F32)
    return out


def _dot_exact_rhs(x, m, parts):
    acc = None
    for p in _split(x, parts):
        t = _dot(p, m)
        acc = t if acc is None else acc + t
    return acc


def _dot_exact_lhs(m, x, parts):
    acc = None
    for p in _split(x, parts):
        t = _dot(m, p)
        acc = t if acc is None else acc + t
    return acc


def _silu(x):
    return x * jax.nn.sigmoid(x)


def _softplus(x):
    return jnp.maximum(x, 0.0) + jnp.log1p(jnp.exp(-jnp.abs(x)))


def _iota(shape, dim):
    return lax.broadcasted_iota(jnp.int32, shape, dim)


def _params(sem):
    return pltpu.CompilerParams(dimension_semantics=sem, vmem_limit_bytes=VMEM_LIMIT)


def _ada_kernel(c_ref, w_ref, b_ref, o_ref):
    ca = _silu(c_ref[...]).astype(BF16)
    o_ref[0] = _dot(ca, w_ref[0].astype(BF16)) + b_ref[0]


def _ada(c, ada_w, ada_b):
    depth, d, n = ada_w.shape
    bsz = c.shape[0]
    tn = 1152
    assert n % tn == 0
    return pl.pallas_call(
        _ada_kernel,
        grid=(depth, n // tn),
        in_specs=[
            pl.BlockSpec((bsz, d), lambda l, j: (0, 0)),
            pl.BlockSpec((1, d, tn), lambda l, j: (l, 0, j)),
            pl.BlockSpec((1, 1, tn), lambda l, j: (l, 0, j)),
        ],
        out_specs=pl.BlockSpec((1, bsz, tn), lambda l, j: (l, 0, j)),
        out_shape=jax.ShapeDtypeStruct((depth, bsz, n), F32),
        compiler_params=_params(("arbitrary", "arbitrary")),
        name="ada_mod",
    )(c, ada_w, ada_b.reshape(depth, 1, n))


def _norm_mod(x, gain, shift, scale):
    y = x * lax.rsqrt(jnp.mean(x * x, axis=-1, keepdims=True) + EPS)
    return (y * gain) * (1.0 + scale) + shift


def _ffn_kernel(x_ref, mod_ref, g_ref, wg_ref, wu_ref, wo_ref, fg_ref, o_ref, h_sc, acc_sc,
                *, base, final):
    j = pl.program_id(1)

    @pl.when(j == 0)
    def _():
        m = mod_ref[0]
        h = _norm_mod(x_ref[...], g_ref[...], m[base:base + 1], m[base + 1:base + 2])
        h_sc[...] = h.astype(BF16)
        acc_sc[...] = jnp.zeros_like(acc_sc)

    h = h_sc[...]
    gate = _dot(h, wg_ref[...])
    up = _dot(h, wu_ref[...])
    a = (_silu(gate) * up).astype(BF16)
    acc_sc[...] += _dot(a, wo_ref[...])

    @pl.when(j == pl.num_programs(1) - 1)
    def _():
        m = mod_ref[0]
        out = x_ref[...] + (HALF_STEP * m[base + 2:base + 3]) * acc_sc[...]
        if final:
            out = (out * lax.rsqrt(jnp.mean(out * out, axis=-1, keepdims=True) + EPS)) * fg_ref[...]
        o_ref[...] = out


def _ffn(xf, mod_l, gain, w_in, w_out, final_gain, *, seq, base, final):
    bt, d = xf.shape
    d_ff = w_out.shape[0]
    tm = 512
    tf = 1408
    assert bt % tm == 0 and seq % tm == 0 and d_ff % tf == 0
    nj = d_ff // tf
    return pl.pallas_call(
        functools.partial(_ffn_kernel, base=base, final=final),
        grid=(bt // tm, nj),
        in_specs=[
            pl.BlockSpec((tm, d), lambda i, j: (i, 0)),
            pl.BlockSpec((1, N_MOD, d), lambda i, j: ((i * tm) // seq, 0, 0)),
            pl.BlockSpec((1, d), lambda i, j: (0, 0)),
            pl.BlockSpec((d, tf), lambda i, j: (0, j)),
            pl.BlockSpec((d, tf), lambda i, j: (0, nj + j)),
            pl.BlockSpec((tf, d), lambda i, j: (j, 0)),
            pl.BlockSpec((1, d), lambda i, j: (0, 0)),
        ],
        out_specs=pl.BlockSpec((tm, d), lambda i, j: (i, 0)),
        out_shape=jax.ShapeDtypeStruct((bt, d), F32),
        scratch_shapes=[pltpu.VMEM((tm, d), BF16), pltpu.VMEM((tm, d), F32)],
        compiler_params=_params(("arbitrary", "arbitrary")),
        name="ffn_half_step",
    )(xf, mod_l, gain.reshape(1, d), w_in, w_in, w_out, final_gain.reshape(1, d))


def _inproj_kernel(x_ref, mod_ref, g_ref, wsb_ref, wgd_ref, whg_ref, wba_ref,
                   sb_ref, gd_ref, hg_ref, ba_ref, *, base):
    m = mod_ref[0]
    h = _norm_mod(x_ref[...], g_ref[...], m[base:base + 1], m[base + 1:base + 2]).astype(BF16)
    sb = _dot(h, wsb_ref[...])
    nq = SB_HEADS * SB_DIM
    col = _iota(sb.shape, 1)
    sb = jnp.where(col < nq, sb * (SB_DIM ** -0.5), sb)
    sb_ref[...] = sb.astype(BF16)
    gd_ref[...] = _dot(h, wgd_ref[...])
    hg_ref[...] = _dot(h, whg_ref[...])
    ba_ref[...] = _dot(h, wba_ref[...])


def _inproj(xf, mod_l, gain, w_sb, w_gd, w_hg, w_ba, *, seq, base):
    bt, d = xf.shape
    tm = 512
    assert bt % tm == 0 and seq % tm == 0
    n_sb, n_gd, n_hg, n_ba = w_sb.shape[1], w_gd.shape[1], w_hg.shape[1], w_ba.shape[1]
    row = lambda n: pl.BlockSpec((tm, n), lambda i: (i, 0))
    full = lambda n: pl.BlockSpec((d, n), lambda i: (0, 0))
    return pl.pallas_call(
        functools.partial(_inproj_kernel, base=base),
        grid=(bt // tm,),
        in_specs=[
            row(d),
            pl.BlockSpec((1, N_MOD, d), lambda i: ((i * tm) // seq, 0, 0)),
            pl.BlockSpec((1, d), lambda i: (0, 0)),
            full(n_sb), full(n_gd), full(n_hg), full(n_ba),
        ],
        out_specs=[row(n_sb), row(n_gd), row(n_hg), row(n_ba)],
        out_shape=[
            jax.ShapeDtypeStruct((bt, n_sb), BF16),
            jax.ShapeDtypeStruct((bt, n_gd), F32),
            jax.ShapeDtypeStruct((bt, n_hg), F32),
            jax.ShapeDtypeStruct((bt, n_ba), F32),
        ],
        compiler_params=_params(("arbitrary",)),
        name="mixer_in_proj",
    )(xf, mod_l, gain.reshape(1, d), w_sb, w_gd, w_hg, w_ba)


SB_TQ = 256
SB_TK = LANES


def _sb_kernel(q_ref, k_ref, v_ref, gain_ref, o_ref, *, seq):
    tq, tk = SB_TQ, SB_TK
    nsub = tq // tk
    lane = _iota((1, LANES), 1)
    jj = _iota((tk, 2 * tk), 0)
    ss = _iota((tk, 2 * tk), 1)
    u2 = jnp.where((jj >= ss) | (ss >= tk), 1.0, 0.0).astype(BF16)
    hh = _iota((LANES, LANES), 0) // SB_DIM
    hc = _iota((LANES, LANES), 1) // SB_DIM
    bd = jnp.where(hh == hc, 1.0, 0.0).astype(BF16)
    rowi = _iota((tq, tk), 0)
    coli = _iota((tq, tk), 1)

    def step(qh, q0, k0, carry, acc, masked):
        kblk = k_ref[pl.ds(k0, tk), :]
        vblk = v_ref[pl.ds(k0, tk), :]
        z = _dot_nt(qh, kblk)
        sp = _softplus(z)
        if masked:
            msk = (k0 + coli) < (q0 + rowi)
            sp = jnp.where(msk, sp, 0.0)
        cs2 = _dot_exact_rhs(sp, u2, 2)
        w = jnp.exp(z - (cs2[:, :tk] + carry))
        if masked:
            w = jnp.where(msk, w, 0.0)
        acc = acc + _dot(w.astype(BF16), vblk)
        carry = carry + cs2[:, tk:]
        return carry, acc

    def qblock(qi, _):
        q0 = pl.multiple_of(qi * tq, tq)
        q2 = q_ref[pl.ds(q0, tq), :]
        out = jnp.zeros((tq, LANES), F32)
        for h in range(LANES // SB_DIM):
            hm = (lane // SB_DIM) == h
            qh = jnp.where(hm, q2, jnp.zeros_like(q2))
            carry = jnp.zeros((tq, tk), F32)
            acc = jnp.zeros((tq, LANES), F32)
            for d in range(nsub):
                k0 = pl.multiple_of(q0 + (nsub - 1 - d) * tk, tk)
                carry, acc = step(qh, q0, k0, carry, acc, True)

            def body(i, ca):
                k0 = pl.multiple_of((qi * nsub - 1 - i) * tk, tk)
                return step(qh, q0, k0, ca[0], ca[1], False)

            carry, acc = lax.fori_loop(0, qi * nsub, body, (carry, acc))
            out = out + jnp.where(hm, acc, 0.0)
        ms = _dot_exact_rhs(out * out, bd, 2) * (1.0 / SB_DIM)
        y = (out * lax.rsqrt(ms + EPS)) * gain_ref[...]
        o_ref[pl.ds(q0, tq), :] = y.astype(BF16)
        return 0

    lax.fori_loop(0, seq // tq, qblock, 0)


def _sb_attention(sb, gain, *, bsz, seq):
    npair = SB_HEADS * SB_DIM // LANES
    gain2 = jnp.tile(gain.reshape(1, SB_DIM), (1, LANES // SB_DIM))
    blk = lambda off: pl.BlockSpec((seq, LANES), lambda b, p: (b, off + p))
    return pl.pallas_call(
        functools.partial(_sb_kernel, seq=seq),
        grid=(bsz, npair),
        in_specs=[blk(0), blk(npair), blk(2 * npair), pl.BlockSpec((1, LANES), lambda b, p: (0, 0))],
        out_specs=pl.BlockSpec((seq, LANES), lambda b, p: (b, p)),
        out_shape=jax.ShapeDtypeStruct((bsz * seq, npair * LANES), BF16),
        compiler_params=_params(("arbitrary", "arbitrary")),
        name="sb_attention",
    )(sb, sb, sb, gain2)


GDN_TT = 256
HALO = SUBLANES


def _gdn_kernel(gd_ref, ba_ref, cw_ref, par_ref, gain_ref, o_ref, ext_sc, qkv_sc, bl_sc, s_sc):
    tt = GDN_TT
    c = GDN_CHUNK
    nh = GDN_HEADS
    dk = GDN_DK
    wq = nh * dk
    nqkv = 3 * wq
    hc = nh * c
    ti = pl.program_id(1)

    @pl.when(ti == 0)
    def _():
        ext_sc[0:HALO, :] = jnp.zeros((HALO, nqkv), F32)
        s_sc[...] = jnp.zeros_like(s_sc)

    ext_sc[HALO:HALO + tt, :] = gd_ref[:, 0:nqkv]
    y = cw_ref[GDN_CONV - 1:GDN_CONV, :] * ext_sc[HALO:HALO + tt, :]
    for i in range(1, GDN_CONV):
        y = y + cw_ref[GDN_CONV - 1 - i:GDN_CONV - i, :] * ext_sc[HALO - i:HALO - i + tt, :]
    ext_sc[0:HALO, :] = ext_sc[tt:tt + HALO, :]
    y = _silu(y)
    for h in range(nh):
        for part, mul in ((0, dk ** -0.5), (1, 1.0)):
            lo = part * wq + h * dk
            xh = y[:, lo:lo + dk]
            nrm = lax.rsqrt(jnp.sum(xh * xh, axis=-1, keepdims=True) + EPS)
            qkv_sc[:, lo:lo + dk] = xh * (nrm * mul) if mul != 1.0 else xh * nrm
    qkv_sc[:, 2 * wq:] = y[:, 2 * wq:]

    ba = ba_ref[...]
    lane = _iota((1, LANES), 1)
    beta_all = jax.nn.sigmoid(ba)
    la_all = -jnp.exp(par_ref[0:1, :]) * _softplus(ba + par_ref[1:2, :])
    for h in range(nh):
        beta_h = jnp.sum(jnp.where(lane == h, beta_all, 0.0), axis=-1, keepdims=True)
        la_h = jnp.sum(jnp.where(lane == nh + h, la_all, 0.0), axis=-1, keepdims=True)
        bl_sc[h] = jnp.broadcast_to(beta_h, (tt, LANES))
        bl_sc[nh + h] = jnp.broadcast_to(la_h, (tt, LANES))

    r = _iota((hc, hc), 0)
    s = _iota((hc, hc), 1)
    same_head = (r // c) == (s // c)
    incl = same_head & (s <= r)
    strict = same_head & (s < r)
    tri = jnp.where(incl, 1.0, 0.0).astype(BF16)
    eye = jnp.where(r == s, 1.0, 0.0)

    def level_mask(m):
        return ((r // (2 * m)) == (s // (2 * m))) & (((r // m) % 2) == 1) & (((s // m) % 2) == 0)

    def chunk(ci, _):
        r0 = pl.multiple_of(ci * c, c)
        rows = pl.ds(r0, c)
        stack = lambda off: jnp.concatenate(
            [qkv_sc[rows, off + h * dk:off + (h + 1) * dk] for h in range(nh)], axis=0)
        q_st, k_st, v_st = stack(0), stack(wq), stack(2 * wq)
        beta_st = jnp.concatenate([bl_sc[h, rows, :] for h in range(nh)], axis=0)
        la_bc = jnp.concatenate([bl_sc[nh + h, rows, :] for h in range(nh)], axis=0)
        g_bc = _dot_exact_lhs(tri, la_bc, 3)
        g_last = jnp.concatenate(
            [jnp.broadcast_to(g_bc[h * c + c - 1:h * c + c, :], (c, LANES)) for h in range(nh)], axis=0)
        g_t = jnp.concatenate([g_bc, g_bc], axis=1)
        g_s = jnp.broadcast_to(g_bc.T[0:1, :], (hc, hc))
        decay = jnp.where(incl, jnp.exp(jnp.where(incl, g_t - g_s, 0.0)), 0.0)
        kb_st = k_st * beta_st
        k_bf = k_st.astype(BF16)
        a_kk = jnp.where(strict, _dot_nt(kb_st.astype(BF16), k_bf) * decay, 0.0)
        a_qk = _dot_nt(q_st.astype(BF16), k_bf) * decay

        x = eye - jnp.where(level_mask(1), a_kk, 0.0)
        m = 2
        while m < c:
            lm = jnp.where(level_mask(m), a_kk, 0.0).astype(BF16)
            xb = x.astype(BF16)
            x = x - _dot(xb, _dot(lm, xb).astype(BF16))
            m *= 2

        eg = jnp.exp(g_bc)
        rhs = jnp.concatenate([v_st * beta_st, kb_st * eg], axis=1).astype(BF16)
        uw = _dot(x.astype(BF16), rhs)
        u_st, w_st = uw[:, :dk], uw[:, dk:]
        q_dec = (q_st * eg).astype(BF16)
        k_dec = (k_st * jnp.exp(g_last - g_bc)).astype(BF16)
        cd = jnp.exp(g_last)

        vn, oi = [], []
        for h in range(nh):
            sl = slice(h * c, (h + 1) * c)
            s_old = s_sc[h]
            s_bf = s_old.astype(BF16)
            v_new = u_st[sl] - _dot(w_st[sl].astype(BF16), s_bf)
            oi.append(_dot(q_dec[sl], s_bf))
            s_sc[h] = s_old * cd[h * c:h * c + 1, :] + _dot_tn(k_dec[sl], v_new.astype(BF16))
            vn.append(v_new)
        v_new_st = jnp.concatenate(vn, axis=0)
        o_st = jnp.concatenate(oi, axis=0) + _dot(a_qk.astype(BF16), v_new_st.astype(BF16))

        o_st = (o_st * lax.rsqrt(jnp.mean(o_st * o_st, axis=-1, keepdims=True) + EPS)) * gain_ref[...]
        o_t = jnp.concatenate([o_st[h * c:(h + 1) * c] for h in range(nh)], axis=1)
        z = gd_ref[rows, nqkv:nqkv + wq]
        o_ref[rows, :] = (o_t * _silu(z)).astype(BF16)
        return 0

    lax.fori_loop(0, tt // c, chunk, 0)


def _gdn(gd, ba, conv_w, a_log, dt_bias, gain, *, bsz, seq):
    tt = GDN_TT
    nt = seq // tt
    wq = GDN_HEADS * GDN_DK
    par = jnp.zeros((SUBLANES, LANES), F32)
    par = par.at[0, GDN_HEADS:2 * GDN_HEADS].set(a_log).at[1, GDN_HEADS:2 * GDN_HEADS].set(dt_bias)
    return pl.pallas_call(
        _gdn_kernel,
        grid=(bsz, nt),
        in_specs=[
            pl.BlockSpec((tt, gd.shape[1]), lambda b, t: (b * nt + t, 0)),
            pl.BlockSpec((tt, LANES), lambda b, t: (b * nt + t, 0)),
            pl.BlockSpec(conv_w.shape, lambda b, t: (0, 0)),
            pl.BlockSpec((SUBLANES, LANES), lambda b, t: (0, 0)),
            pl.BlockSpec((1, GDN_DV), lambda b, t: (0, 0)),
        ],
        out_specs=pl.BlockSpec((tt, wq), lambda b, t: (b * nt + t, 0)),
        out_shape=jax.ShapeDtypeStruct((bsz * seq, wq), BF16),
        scratch_shapes=[
            pltpu.VMEM((tt + HALO, 3 * wq), F32),
            pltpu.VMEM((tt, 3 * wq), F32),
            pltpu.VMEM((2 * GDN_HEADS, tt, LANES), F32),
            pltpu.VMEM((GDN_HEADS, GDN_DK, GDN_DV), F32),
        ],
        compiler_params=_params(("arbitrary", "arbitrary")),
        name="gated_deltanet",
    )(gd, ba, conv_w, par, gain.reshape(1, GDN_DV))


HG_TT = 256


def _hg_kernel(hg_ref, lbl_ref, gain_ref, o_ref, st_sc, oi_sc, *, layer):
    tt = HG_TT
    c = HG_CHUNK
    w = HG_HEADS * HG_DK
    ti = pl.program_id(1)

    @pl.when(ti == 0)
    def _():
        st_sc[...] = jnp.zeros_like(st_sc)

    lg = lbl_ref[...]
    e = jnp.exp(lg - jnp.max(lg, axis=0, keepdims=True))
    p = e / jnp.sum(e, axis=0, keepdims=True)
    lb = jnp.zeros((1, w), F32)
    for i in range(1, layer + 1):
        lb = lb + p[i:i + 1, :]
    log_lb = jnp.log(lb)
    log1m_lb = jnp.log1p(-lb)

    hr = _iota((w, w), 0) // HG_DK
    hcn = _iota((w, w), 1) // HG_DK
    same_head = hr == hcn
    bd = jnp.where(same_head, 1.0, 0.0).astype(BF16)
    tr = _iota((c, c), 0)
    ts = _iota((c, c), 1)
    tri = jnp.where(ts <= tr, 1.0, 0.0).astype(BF16)

    def chunk(ci, _):
        rows = pl.ds(pl.multiple_of(ci * c, c), c)
        q = _silu(hg_ref[rows, 0:w])
        f_pre = hg_ref[rows, w:2 * w]
        v = hg_ref[rows, 2 * w:3 * w]
        gate = hg_ref[rows, 3 * w:4 * w]
        log_sig = jnp.minimum(f_pre, 0.0) - jnp.log1p(jnp.exp(-jnp.abs(f_pre)))
        bterm = log1m_lb + log_sig
        lf = jnp.maximum(log_lb, bterm) + jnp.log1p(jnp.exp(-jnp.abs(log_lb - bterm)))
        k_in = (1.0 - lb) * jax.nn.sigmoid(-f_pre)
        b = _dot_exact_lhs(tri, lf, 3)
        b_last = b[c - 1:c, :]

        st = st_sc[...]
        o_inter = _dot_nt((q * jnp.exp(b)).astype(BF16), st.astype(BF16))
        k_dec = (k_in * jnp.exp(b_last - b)).astype(BF16)
        upd = _dot_tn(v.astype(BF16), k_dec)
        st_sc[...] = st * jnp.exp(b_last) + jnp.where(same_head, upd, 0.0)

        oi_sc[...] = o_inter
        for s in range(c):
            t0 = (s // SUBLANES) * SUBLANES
            diff = b[t0:, :] - b[s:s + 1, :]
            ok = (t0 + _iota((c - t0, 1), 0)) >= s
            ex = jnp.where(ok, jnp.exp(jnp.where(ok, diff, 0.0)), 0.0)
            wgt = (q[t0:, :] * ex) * k_in[s:s + 1, :]
            cw = _dot(wgt.astype(BF16), bd)
            oi_sc[t0:, :] += cw * v[s:s + 1, :]

        o = oi_sc[...]
        ms = _dot_exact_rhs(o * o, bd, 2) * (1.0 / HG_DV)
        y = (o * lax.rsqrt(ms + EPS)) * gain_ref[...]
        o_ref[rows, :] = (y * _silu(gate)).astype(BF16)
        return 0

    lax.fori_loop(0, tt // c, chunk, 0)


def _hgrn2(hg, lb_logits, gain, *, bsz, seq, layer):
    tt = HG_TT
    nt = seq // tt
    w = HG_HEADS * HG_DV
    gain4 = jnp.tile(gain.reshape(1, HG_DV), (1, HG_HEADS))
    return pl.pallas_call(
        functools.partial(_hg_kernel, layer=layer),
        grid=(bsz, nt),
        in_specs=[
            pl.BlockSpec((tt, hg.shape[1]), lambda b, t: (b * nt + t, 0)),
            pl.BlockSpec(lb_logits.shape, lambda b, t: (0, 0)),
            pl.BlockSpec((1, w), lambda b, t: (0, 0)),
        ],
        out_specs=pl.BlockSpec((tt, w), lambda b, t: (b * nt + t, 0)),
        out_shape=jax.ShapeDtypeStruct((bsz * seq, w), BF16),
        scratch_shapes=[pltpu.VMEM((w, w), F32), pltpu.VMEM((HG_CHUNK, w), F32)],
        compiler_params=_params(("arbitrary", "arbitrary")),
        name="hgrn2",
    )(hg, lb_logits, gain4)


def _outproj_kernel(x_ref, mod_ref, osb_ref, ogd_ref, ohg_ref, wsb_ref, wgd_ref, whg_ref, o_ref, *, base):
    y = _dot(osb_ref[...], wsb_ref[...]) + _dot(ogd_ref[...], wgd_ref[...]) + _dot(ohg_ref[...], whg_ref[...])
    m = mod_ref[0]
    o_ref[...] = x_ref[...] + m[base:base + 1] * y


def _outproj(xf, mod_l, o_sb, o_gd, o_hg, w_sb, w_gd, w_hg, *, seq, base):
    bt, d = xf.shape
    tm = 512
    row = lambda n: pl.BlockSpec((tm, n), lambda i: (i, 0))
    full = lambda a: pl.BlockSpec(a.shape, lambda i: (0, 0))
    return pl.pallas_call(
        functools.partial(_outproj_kernel, base=base),
        grid=(bt // tm,),
        in_specs=[
            row(d),
            pl.BlockSpec((1, N_MOD, d), lambda i: ((i * tm) // seq, 0, 0)),
            row(o_sb.shape[1]), row(o_gd.shape[1]), row(o_hg.shape[1]),
            full(w_sb), full(w_gd), full(w_hg),
        ],
        out_specs=row(d),
        out_shape=jax.ShapeDtypeStruct((bt, d), F32),
        compiler_params=_params(("arbitrary",)),
        name="mixer_out_proj",
    )(xf, mod_l, o_sb, o_gd, o_hg, w_sb, w_gd, w_hg)


def kernel(x, c, ffn1_norm, ffn1_w_in, ffn1_w_out, mix_norm, mix_w_in, gdn_conv_w, gdn_A_log, gdn_dt_bias,
           sb_out_norm, gdn_out_norm, hg_out_norm, hg_lb_logits, mix_w_out, ffn2_norm, ffn2_w_in, ffn2_w_out,
           ada_w, ada_b, final_norm):
    bsz, seq, d = x.shape
    depth = ada_w.shape[0]
    sb_w = SB_HEADS * SB_DIM
    gd_wk = GDN_HEADS * GDN_DK
    gd_wv = GDN_HEADS * GDN_DV
    hg_w = HG_HEADS * HG_DK

    mod = _ada(c, ada_w, ada_b)
    xf = x.reshape(bsz * seq, d)
    for l in range(depth):
        mod_l = mod[l].reshape(bsz, N_MOD, d)
        xf = _ffn(xf, mod_l, ffn1_norm[l], ffn1_w_in[l].astype(BF16), ffn1_w_out[l].astype(BF16),
                  final_norm, seq=seq, base=0, final=False)

        wi = mix_w_in[l].astype(BF16)
        o_gd = 3 * sb_w
        o_b = o_gd + 3 * gd_wk + gd_wv
        o_hg = o_b + 2 * GDN_HEADS
        w_ba = jnp.pad(wi[:, o_b:o_hg], ((0, 0), (0, LANES - 2 * GDN_HEADS)))
        sb, gd, hg, ba = _inproj(xf, mod_l, mix_norm[l], wi[:, :o_gd], wi[:, o_gd:o_b], wi[:, o_hg:], w_ba,
                                 seq=seq, base=3)

        o_sb = _sb_attention(sb, sb_out_norm[l], bsz=bsz, seq=seq)
        o_gdn = _gdn(gd, ba, gdn_conv_w[l], gdn_A_log[l], gdn_dt_bias[l], gdn_out_norm[l], bsz=bsz, seq=seq)
        o_hgr = _hgrn2(hg, hg_lb_logits, hg_out_norm[l], bsz=bsz, seq=seq, layer=l)

        wo = mix_w_out[l].astype(BF16)
        xf = _outproj(xf, mod_l, o_sb, o_gdn, o_hgr, wo[:sb_w], wo[sb_w:sb_w + gd_wv], wo[sb_w + gd_wv:],
                      seq=seq, base=5)

        xf = _ffn(xf, mod_l, ffn2_norm[l], ffn2_w_in[l].astype(BF16), ffn2_w_out[l].astype(BF16),
                  final_norm, seq=seq, base=6, final=(l == depth - 1))
    return xf.reshape(bsz, seq, d)
```

```python
import functools

import jax
import jax.numpy as jnp
from jax import lax
from jax.experimental import pallas as pl
from jax.experimental.pallas import tpu as pltpu

F32 = jnp.float32
BF16 = jnp.bfloat16

EPS = 1e-6
HALF_STEP = 0.5
N_MOD = 9

SB_HEADS, SB_DIM = 4, 64
GDN_HEADS, GDN_DK, GDN_DV, GDN_CONV, GDN_CHUNK = 4, 128, 128, 4, 64
HG_HEADS, HG_DK, HG_DV, HG_CHUNK = 4, 64, 64, 64

LANES = 128
SUBLANES = 8
VMEM_LIMIT = 56 * 1024 * 1024

_NT = (((1,), (1,)), ((), ()))
_TN = (((0,), (0,)), ((), ()))


def _dot(a, b):
    return jnp.dot(a, b, preferred_element_type=F32)


def _dot_nt(a, b):
    return lax.dot_general(a, b, _NT, preferred_element_type=F32)


def _dot_tn(a, b):
    return lax.dot_general(a, b, _TN, preferred_element_type=F32)


def _split(x, parts):
    out = []
    r = x
    for _ in range(parts):
        p = r.astype(BF16)
        out.append(p)
        r = r - p.astype(F32)
    return out


def _dot_exact_rhs(x, m, parts):
    acc = None
    for p in _split(x, parts):
        t = _dot(p, m)
        acc = t if acc is None else acc + t
    return acc


def _dot_exact_lhs(m, x, parts):
    acc = None
    for p in _split(x, parts):
        t = _dot(m, p)
        acc = t if acc is None else acc + t
    return acc


def _silu(x):
    return x * jax.nn.sigmoid(x)


def _softplus(x):
    return jnp.maximum(x, 0.0) + jnp.log1p(jnp.exp(-jnp.abs(x)))


def _softplus_abs(x):
    return jnp.maximum(x, 0.0) + jnp.log(1.0 + jnp.exp(-jnp.abs(x)))


def _iota(shape, dim):
    return lax.broadcasted_iota(jnp.int32, shape, dim)


def _params(sem):
    return pltpu.CompilerParams(dimension_semantics=sem, vmem_limit_bytes=VMEM_LIMIT)


def _ada_kernel(c_ref, w_ref, b_ref, o_ref):
    ca = _silu(c_ref[...]).astype(BF16)
    o_ref[0] = _dot(ca, w_ref[0].astype(BF16)) + b_ref[0]


def _ada(c, ada_w, ada_b):
    depth, d, n = ada_w.shape
    bsz = c.shape[0]
    tn = 1152
    assert n % tn == 0
    return pl.pallas_call(
        _ada_kernel,
        grid=(depth, n // tn),
        in_specs=[
            pl.BlockSpec((bsz, d), lambda l, j: (0, 0)),
            pl.BlockSpec((1, d, tn), lambda l, j: (l, 0, j)),
            pl.BlockSpec((1, 1, tn), lambda l, j: (l, 0, j)),
        ],
        out_specs=pl.BlockSpec((1, bsz, tn), lambda l, j: (l, 0, j)),
        out_shape=jax.ShapeDtypeStruct((depth, bsz, n), F32),
        compiler_params=_params(("arbitrary", "arbitrary")),
        name="ada_mod",
    )(c, ada_w, ada_b.reshape(depth, 1, n))


def _norm_mod(x, gain, shift, scale):
    y = x * lax.rsqrt(jnp.mean(x * x, axis=-1, keepdims=True) + EPS)
    return (y * gain) * (1.0 + scale) + shift


def _ffn_kernel(x_ref, mod_ref, g_ref, wg_ref, wu_ref, wo_ref, fg_ref, o_ref, h_sc, acc_sc,
                *, base, final):
    j = pl.program_id(1)

    @pl.when(j == 0)
    def _():
        m = mod_ref[0]
        h = _norm_mod(x_ref[...], g_ref[...], m[base:base + 1], m[base + 1:base + 2])
        h_sc[...] = h.astype(BF16)
        acc_sc[...] = jnp.zeros_like(acc_sc)

    h = h_sc[...]
    gate = _dot(h, wg_ref[...])
    up = _dot(h, wu_ref[...])
    a = (_silu(gate) * up).astype(BF16)
    acc_sc[...] += _dot(a, wo_ref[...])

    @pl.when(j == pl.num_programs(1) - 1)
    def _():
        m = mod_ref[0]
        out = x_ref[...] + (HALF_STEP * m[base + 2:base + 3]) * acc_sc[...]
        if final:
            out = (out * lax.rsqrt(jnp.mean(out * out, axis=-1, keepdims=True) + EPS)) * fg_ref[...]
        o_ref[...] = out


def _ffn(xf, mod_l, gain, w_in, w_out, final_gain, *, seq, base, final):
    bt, d = xf.shape
    d_ff = w_out.shape[0]
    tm = 512
    tf = 1408
    assert bt % tm == 0 and seq % tm == 0 and d_ff % tf == 0
    nj = d_ff // tf
    return pl.pallas_call(
        functools.partial(_ffn_kernel, base=base, final=final),
        grid=(bt // tm, nj),
        in_specs=[
            pl.BlockSpec((tm, d), lambda i, j: (i, 0)),
            pl.BlockSpec((1, N_MOD, d), lambda i, j: ((i * tm) // seq, 0, 0)),
            pl.BlockSpec((1, d), lambda i, j: (0, 0)),
            pl.BlockSpec((d, tf), lambda i, j: (0, j)),
            pl.BlockSpec((d, tf), lambda i, j: (0, nj + j)),
            pl.BlockSpec((tf, d), lambda i, j: (j, 0)),
            pl.BlockSpec((1, d), lambda i, j: (0, 0)),
        ],
        out_specs=pl.BlockSpec((tm, d), lambda i, j: (i, 0)),
        out_shape=jax.ShapeDtypeStruct((bt, d), F32),
        scratch_shapes=[pltpu.VMEM((tm, d), BF16), pltpu.VMEM((tm, d), F32)],
        compiler_params=_params(("arbitrary", "arbitrary")),
        name="ffn_half_step",
    )(xf, mod_l, gain.reshape(1, d), w_in, w_in, w_out, final_gain.reshape(1, d))


def _inproj_kernel(x_ref, mod_ref, g_ref, wsb_ref, wgd_ref, whg_ref, wba_ref,
                   sb_ref, gd_ref, hg_ref, ba_ref, *, base):
    m = mod_ref[0]
    h = _norm_mod(x_ref[...], g_ref[...], m[base:base + 1], m[base + 1:base + 2]).astype(BF16)
    sb = _dot(h, wsb_ref[...])
    nq = SB_HEADS * SB_DIM
    col = _iota(sb.shape, 1)
    sb = jnp.where(col < nq, sb * (SB_DIM ** -0.5), sb)
    sb_ref[...] = sb.astype(BF16)
    gd_ref[...] = _dot(h, wgd_ref[...])
    hg_ref[...] = _dot(h, whg_ref[...])
    ba_ref[...] = _dot(h, wba_ref[...])


def _inproj(xf, mod_l, gain, w_sb, w_gd, w_hg, w_ba, *, seq, base):
    bt, d = xf.shape
    tm = 512
    assert bt % tm == 0 and seq % tm == 0
    n_sb, n_gd, n_hg, n_ba = w_sb.shape[1], w_gd.shape[1], w_hg.shape[1], w_ba.shape[1]
    row = lambda n: pl.BlockSpec((tm, n), lambda i: (i, 0))
    full = lambda n: pl.BlockSpec((d, n), lambda i: (0, 0))
    return pl.pallas_call(
        functools.partial(_inproj_kernel, base=base),
        grid=(bt // tm,),
        in_specs=[
            row(d),
            pl.BlockSpec((1, N_MOD, d), lambda i: ((i * tm) // seq, 0, 0)),
            pl.BlockSpec((1, d), lambda i: (0, 0)),
            full(n_sb), full(n_gd), full(n_hg), full(n_ba),
        ],
        out_specs=[row(n_sb), row(n_gd), row(n_hg), row(n_ba)],
        out_shape=[
            jax.ShapeDtypeStruct((bt, n_sb), BF16),
            jax.ShapeDtypeStruct((bt, n_gd), F32),
            jax.ShapeDtypeStruct((bt, n_hg), F32),
            jax.ShapeDtypeStruct((bt, n_ba), F32),
        ],
        compiler_params=_params(("arbitrary",)),
        name="mixer_in_proj",
    )(xf, mod_l, gain.reshape(1, d), w_sb, w_gd, w_hg, w_ba)


SB_TQ = 512
SB_TC = 256


def _sb_kernel(q_ref, k_ref, v_ref, gain_ref, o_ref, *, seq):
    tq, tc = SB_TQ, SB_TC
    lane = _iota((1, LANES), 1)
    u = jnp.where(_iota((tc, tc), 0) >= _iota((tc, tc), 1), 1.0, 0.0).astype(BF16)
    hh = _iota((LANES, LANES), 0) // SB_DIM
    hc = _iota((LANES, LANES), 1) // SB_DIM
    bd = jnp.where(hh == hc, 1.0, 0.0).astype(BF16)
    causal = _iota((tq, tq), 1) < _iota((tq, tq), 0)

    def step(qh, k0, carry, acc, masked):
        kblk = k_ref[pl.ds(k0, tq), :]
        vblk = v_ref[pl.ds(k0, tq), :]
        z = _dot_nt(qh, kblk)
        sp = _softplus_abs(z)
        if masked:
            sp = jnp.where(causal, sp, 0.0)
        ws = [None] * (tq // tc)
        for g in reversed(range(tq // tc)):
            spg = sp[:, g * tc:(g + 1) * tc]
            r = _dot_exact_rhs(spg, u, 2) + carry
            ws[g] = jnp.exp(z[:, g * tc:(g + 1) * tc] - r)
            carry = carry + jnp.sum(spg, axis=-1, keepdims=True)
        w = jnp.concatenate(ws, axis=1)
        if masked:
            w = jnp.where(causal, w, 0.0)
        acc = acc + _dot(w.astype(BF16), vblk)
        return carry, acc

    def qblock(qi, _):
        q0 = pl.multiple_of(qi * tq, tq)
        q2 = q_ref[pl.ds(q0, tq), :]
        out = jnp.zeros((tq, LANES), F32)
        for h in range(LANES // SB_DIM):
            hm = (lane // SB_DIM) == h
            qh = jnp.where(hm, q2, jnp.zeros_like(q2))
            carry = jnp.zeros((tq, 1), F32)
            acc = jnp.zeros((tq, LANES), F32)
            carry, acc = step(qh, q0, carry, acc, True)

            def body(i, ca):
                k0 = pl.multiple_of((qi - 1 - i) * tq, tq)
                return step(qh, k0, ca[0], ca[1], False)

            carry, acc = lax.fori_loop(0, qi, body, (carry, acc))
            out = out + jnp.where(hm, acc, 0.0)
        ms = _dot_exact_rhs(out * out, bd, 2) * (1.0 / SB_DIM)
        y = (out * lax.rsqrt(ms + EPS)) * gain_ref[...]
        o_ref[pl.ds(q0, tq), :] = y.astype(BF16)
        return 0

    lax.fori_loop(0, seq // tq, qblock, 0)


def _sb_attention(sb, gain, *, bsz, seq):
    npair = SB_HEADS * SB_DIM // LANES
    gain2 = jnp.tile(gain.reshape(1, SB_DIM), (1, LANES // SB_DIM))
    blk = lambda off: pl.BlockSpec((seq, LANES), lambda b, p: (b, off + p))
    return pl.pallas_call(
        functools.partial(_sb_kernel, seq=seq),
        grid=(bsz, npair),
        in_specs=[blk(0), blk(npair), blk(2 * npair), pl.BlockSpec((1, LANES), lambda b, p: (0, 0))],
        out_specs=pl.BlockSpec((seq, LANES), lambda b, p: (b, p)),
        out_shape=jax.ShapeDtypeStruct((bsz * seq, npair * LANES), BF16),
        compiler_params=_params(("arbitrary", "arbitrary")),
        name="sb_attention",
    )(sb, sb, sb, gain2)


GDN_TT = 256
HALO = SUBLANES


def _gdn_kernel(gd_ref, ba_ref, cw_ref, par_ref, gain_ref, o_ref, ext_sc, qkv_sc, bl_sc, s_sc):
    tt = GDN_TT
    c = GDN_CHUNK
    nh = GDN_HEADS
    dk = GDN_DK
    wq = nh * dk
    nqkv = 3 * wq
    hc = nh * c
    ti = pl.program_id(1)

    @pl.when(ti == 0)
    def _():
        ext_sc[0:HALO, :] = jnp.zeros((HALO, nqkv), F32)
        s_sc[...] = jnp.zeros_like(s_sc)

    ext_sc[HALO:HALO + tt, :] = gd_ref[:, 0:nqkv]
    y = cw_ref[GDN_CONV - 1:GDN_CONV, :] * ext_sc[HALO:HALO + tt, :]
    for i in range(1, GDN_CONV):
        y = y + cw_ref[GDN_CONV - 1 - i:GDN_CONV - i, :] * ext_sc[HALO - i:HALO - i + tt, :]
    ext_sc[0:HALO, :] = ext_sc[tt:tt + HALO, :]
    y = _silu(y)
    for h in range(nh):
        for part, mul in ((0, dk ** -0.5), (1, 1.0)):
            lo = part * wq + h * dk
            xh = y[:, lo:lo + dk]
            nrm = lax.rsqrt(jnp.sum(xh * xh, axis=-1, keepdims=True) + EPS)
            qkv_sc[:, lo:lo + dk] = xh * (nrm * mul) if mul != 1.0 else xh * nrm
    qkv_sc[:, 2 * wq:] = y[:, 2 * wq:]

    ba = ba_ref[...]
    lane = _iota((1, LANES), 1)
    beta_all = jax.nn.sigmoid(ba)
    la_all = -jnp.exp(par_ref[0:1, :]) * _softplus(ba + par_ref[1:2, :])
    for h in range(nh):
        beta_h = jnp.sum(jnp.where(lane == h, beta_all, 0.0), axis=-1, keepdims=True)
        la_h = jnp.sum(jnp.where(lane == nh + h, la_all, 0.0), axis=-1, keepdims=True)
        bl_sc[h] = jnp.broadcast_to(beta_h, (tt, LANES))
        bl_sc[nh + h] = jnp.broadcast_to(la_h, (tt, LANES))

    r = _iota((hc, hc), 0)
    s = _iota((hc, hc), 1)
    same_head = (r // c) == (s // c)
    incl = same_head & (s <= r)
    strict = same_head & (s < r)
    tri = jnp.where(incl, 1.0, 0.0).astype(BF16)
    eye = jnp.where(r == s, 1.0, 0.0)

    rxs = r ^ s
    lmask = {m: jnp.where((s < r) & (rxs >= m) & (rxs < 2 * m), 1.0, 0.0).astype(BF16)
             for m in (1, 2, 4, 8, 16, 32)}
    assert 2 * max(lmask) == c

    nch = tt // c
    rows_of = [pl.ds(ci * c, c) for ci in range(nch)]

    def stack(rows, off):
        return jnp.concatenate([qkv_sc[rows, off + h * dk:off + (h + 1) * dk] for h in range(nh)], axis=0)

    q_st = [stack(rw, 0) for rw in rows_of]
    k_st = [stack(rw, wq) for rw in rows_of]
    v_st = [stack(rw, 2 * wq) for rw in rows_of]
    beta_st = [jnp.concatenate([bl_sc[h, rw, :] for h in range(nh)], axis=0) for rw in rows_of]
    la_bc = [jnp.concatenate([bl_sc[nh + h, rw, :] for h in range(nh)], axis=0) for rw in rows_of]
    g_bc = [_dot_exact_lhs(tri, la, 3) for la in la_bc]
    g_last = [jnp.concatenate(
        [jnp.broadcast_to(g[h * c + c - 1:h * c + c, :], (c, LANES)) for h in range(nh)], axis=0)
        for g in g_bc]
    ex = [jnp.exp(jnp.concatenate([g, g], axis=1) - jnp.broadcast_to(g.T[0:1, :], (hc, hc))) for g in g_bc]
    kb_st = [k * b for k, b in zip(k_st, beta_st)]
    k_bf = [k.astype(BF16) for k in k_st]
    a_kk = [(_dot_nt(kb.astype(BF16), kf) * jnp.where(strict, e, 0.0)).astype(BF16)
            for kb, kf, e in zip(kb_st, k_bf, ex)]
    a_qk = [(_dot_nt(q.astype(BF16), kf) * jnp.where(incl, e, 0.0)).astype(BF16)
            for q, kf, e in zip(q_st, k_bf, ex)]

    xs = [eye - (a * lmask[1]).astype(F32) for a in a_kk]
    m = 2
    while m < c:
        xb = [x.astype(BF16) for x in xs]
        ys = [_dot(a * lmask[m], b).astype(BF16) for a, b in zip(a_kk, xb)]
        xs = [x - _dot(b, y) for x, b, y in zip(xs, xb, ys)]
        m *= 2

    eg = [jnp.exp(g) for g in g_bc]
    uw = [_dot(x.astype(BF16), jnp.concatenate([v * b, kb * e], axis=1).astype(BF16))
          for x, v, b, kb, e in zip(xs, v_st, beta_st, kb_st, eg)]
    q_decs = [(q * e).astype(BF16) for q, e in zip(q_st, eg)]
    k_decs = [(k * jnp.exp(gl - g)).astype(BF16) for k, gl, g in zip(k_st, g_last, g_bc)]
    cds = [jnp.exp(gl) for gl in g_last]

    for ci in range(nch):
        rows = rows_of[ci]
        u_st, w_st = uw[ci][:, :dk], uw[ci][:, dk:]
        q_dec, k_dec, cd = q_decs[ci], k_decs[ci], cds[ci]
        vn, oi = [], []
        for h in range(nh):
            sl = slice(h * c, (h + 1) * c)
            s_old = s_sc[h]
            s_bf = s_old.astype(BF16)
            v_new = u_st[sl] - _dot(w_st[sl].astype(BF16), s_bf)
            oi.append(_dot(q_dec[sl], s_bf))
            s_sc[h] = s_old * cd[h * c:h * c + 1, :] + _dot_tn(k_dec[sl], v_new.astype(BF16))
            vn.append(v_new)
        v_new_st = jnp.concatenate(vn, axis=0)
        o_st = jnp.concatenate(oi, axis=0) + _dot(a_qk[ci], v_new_st.astype(BF16))

        o_st = (o_st * lax.rsqrt(jnp.mean(o_st * o_st, axis=-1, keepdims=True) + EPS)) * gain_ref[...]
        o_t = jnp.concatenate([o_st[h * c:(h + 1) * c] for h in range(nh)], axis=1)
        z = gd_ref[rows, nqkv:nqkv + wq]
        o_ref[rows, :] = (o_t * _silu(z)).astype(BF16)


def _gdn(gd, ba, conv_w, a_log, dt_bias, gain, *, bsz, seq):
    tt = GDN_TT
    nt = seq // tt
    wq = GDN_HEADS * GDN_DK
    par = jnp.zeros((SUBLANES, LANES), F32)
    par = par.at[0, GDN_HEADS:2 * GDN_HEADS].set(a_log).at[1, GDN_HEADS:2 * GDN_HEADS].set(dt_bias)
    return pl.pallas_call(
        _gdn_kernel,
        grid=(bsz, nt),
        in_specs=[
            pl.BlockSpec((tt, gd.shape[1]), lambda b, t: (b * nt + t, 0)),
            pl.BlockSpec((tt, LANES), lambda b, t: (b * nt + t, 0)),
            pl.BlockSpec(conv_w.shape, lambda b, t: (0, 0)),
            pl.BlockSpec((SUBLANES, LANES), lambda b, t: (0, 0)),
            pl.BlockSpec((1, GDN_DV), lambda b, t: (0, 0)),
        ],
        out_specs=pl.BlockSpec((tt, wq), lambda b, t: (b * nt + t, 0)),
        out_shape=jax.ShapeDtypeStruct((bsz * seq, wq), BF16),
        scratch_shapes=[
            pltpu.VMEM((tt + HALO, 3 * wq), F32),
            pltpu.VMEM((tt, 3 * wq), F32),
            pltpu.VMEM((2 * GDN_HEADS, tt, LANES), F32),
            pltpu.VMEM((GDN_HEADS, GDN_DK, GDN_DV), F32),
        ],
        compiler_params=_params(("arbitrary", "arbitrary")),
        name="gated_deltanet",
    )(gd, ba, conv_w, par, gain.reshape(1, GDN_DV))


HG_TT = 256


def _hg_kernel(hg_ref, lbl_ref, gain_ref, o_ref, st_sc, *, layer):
    tt = HG_TT
    c = HG_CHUNK
    w = HG_HEADS * HG_DK
    ti = pl.program_id(1)

    @pl.when(ti == 0)
    def _():
        st_sc[...] = jnp.zeros_like(st_sc)

    lg = lbl_ref[...]
    e = jnp.exp(lg - jnp.max(lg, axis=0, keepdims=True))
    p = e / jnp.sum(e, axis=0, keepdims=True)
    lb = jnp.zeros((1, w), F32)
    for i in range(1, layer + 1):
        lb = lb + p[i:i + 1, :]
    log_lb = jnp.log(lb)
    log1m_lb = jnp.log1p(-lb)

    hr = _iota((w, w), 0) // HG_DK
    hcn = _iota((w, w), 1) // HG_DK
    same_head = hr == hcn
    bd = jnp.where(same_head, 1.0, 0.0).astype(BF16)
    tr = _iota((c, c), 0)
    ts = _iota((c, c), 1)
    tri = jnp.where(ts <= tr, 1.0, 0.0).astype(BF16)

    def chunk(ci, _):
        rows = pl.ds(pl.multiple_of(ci * c, c), c)
        q = _silu(hg_ref[rows, 0:w])
        f_pre = hg_ref[rows, w:2 * w]
        v = hg_ref[rows, 2 * w:3 * w]
        gate = hg_ref[rows, 3 * w:4 * w]
        log_sig = jnp.minimum(f_pre, 0.0) - jnp.log1p(jnp.exp(-jnp.abs(f_pre)))
        bterm = log1m_lb + log_sig
        lf = jnp.maximum(log_lb, bterm) + jnp.log1p(jnp.exp(-jnp.abs(log_lb - bterm)))
        k_in = (1.0 - lb) * jax.nn.sigmoid(-f_pre)
        b = _dot_exact_lhs(tri, lf, 3)
        b_last = b[c - 1:c, :]

        st = st_sc[...]
        o_inter = _dot_nt((q * jnp.exp(b)).astype(BF16), st.astype(BF16))
        k_dec = (k_in * jnp.exp(b_last - b)).astype(BF16)
        upd = _dot_tn(v.astype(BF16), k_dec)
        st_sc[...] = st * jnp.exp(b_last) + jnp.where(same_head, upd, 0.0)

        nt8 = c // SUBLANES
        o_tiles = [o_inter[j * SUBLANES:(j + 1) * SUBLANES] for j in range(nt8)]
        sub = _iota((SUBLANES, 1), 0)
        for g in range(nt8):
            t0 = g * SUBLANES
            n = c - t0
            qt, bt = q[t0:, :], b[t0:, :]
            ws = []
            for i in range(SUBLANES):
                s = t0 + i
                diff = bt - b[s:s + 1, :]
                ok = sub >= i
                ex = jnp.where(ok, jnp.exp(jnp.where(ok, diff[:SUBLANES], 0.0)), 0.0)
                if n > SUBLANES:
                    ex = jnp.concatenate([ex, jnp.exp(diff[SUBLANES:])], axis=0)
                ws.append((qt * ex) * k_in[s:s + 1, :])
            cw = _dot(jnp.concatenate(ws, axis=0).astype(BF16), bd)
            for i in range(SUBLANES):
                contrib = cw[i * n:(i + 1) * n] * v[t0 + i:t0 + i + 1, :]
                for j in range(g, nt8):
                    o_tiles[j] = o_tiles[j] + contrib[(j - g) * SUBLANES:(j - g + 1) * SUBLANES]
        o = jnp.concatenate(o_tiles, axis=0)
        ms = _dot_exact_rhs(o * o, bd, 2) * (1.0 / HG_DV)
        y = (o * lax.rsqrt(ms + EPS)) * gain_ref[...]
        o_ref[rows, :] = (y * _silu(gate)).astype(BF16)
        return 0

    lax.fori_loop(0, tt // c, chunk, 0)


def _hgrn2(hg, lb_logits, gain, *, bsz, seq, layer):
    tt = HG_TT
    nt = seq // tt
    w = HG_HEADS * HG_DV
    gain4 = jnp.tile(gain.reshape(1, HG_DV), (1, HG_HEADS))
    return pl.pallas_call(
        functools.partial(_hg_kernel, layer=layer),
        grid=(bsz, nt),
        in_specs=[
            pl.BlockSpec((tt, hg.shape[1]), lambda b, t: (b * nt + t, 0)),
            pl.BlockSpec(lb_logits.shape, lambda b, t: (0, 0)),
            pl.BlockSpec((1, w), lambda b, t: (0, 0)),
        ],
        out_specs=pl.BlockSpec((tt, w), lambda b, t: (b * nt + t, 0)),
        out_shape=jax.ShapeDtypeStruct((bsz * seq, w), BF16),
        scratch_shapes=[pltpu.VMEM((w, w), F32)],
        compiler_params=_params(("arbitrary", "arbitrary")),
        name="hgrn2",
    )(hg, lb_logits, gain4)


def _outproj_kernel(x_ref, mod_ref, osb_ref, ogd_ref, ohg_ref, wsb_ref, wgd_ref, whg_ref, o_ref, *, base):
    y = _dot(osb_ref[...], wsb_ref[...]) + _dot(ogd_ref[...], wgd_ref[...]) + _dot(ohg_ref[...], whg_ref[...])
    m = mod_ref[0]
    o_ref[...] = x_ref[...] + m[base:base + 1] * y


def _outproj(xf, mod_l, o_sb, o_gd, o_hg, w_sb, w_gd, w_hg, *, seq, base):
    bt, d = xf.shape
    tm = 512
    row = lambda n: pl.BlockSpec((tm, n), lambda i: (i, 0))
    full = lambda a: pl.BlockSpec(a.shape, lambda i: (0, 0))
    return pl.pallas_call(
        functools.partial(_outproj_kernel, base=base),
        grid=(bt // tm,),
        in_specs=[
            row(d),
            pl.BlockSpec((1, N_MOD, d), lambda i: ((i * tm) // seq, 0, 0)),
            row(o_sb.shape[1]), row(o_gd.shape[1]), row(o_hg.shape[1]),
            full(w_sb), full(w_gd), full(w_hg),
        ],
        out_specs=row(d),
        out_shape=jax.ShapeDtypeStruct((bt, d), F32),
        compiler_params=_params(("arbitrary",)),
        name="mixer_out_proj",
    )(xf, mod_l, o_sb, o_gd, o_hg, w_sb, w_gd, w_hg)


def kernel(x, c, ffn1_norm, ffn1_w_in, ffn1_w_out, mix_norm, mix_w_in, gdn_conv_w, gdn_A_log, gdn_dt_bias,
           sb_out_norm, gdn_out_norm, hg_out_norm, hg_lb_logits, mix_w_out, ffn2_norm, ffn2_w_in, ffn2_w_out,
           ada_w, ada_b, final_norm):
    bsz, seq, d = x.shape
    depth = ada_w.shape[0]
    sb_w = SB_HEADS * SB_DIM
    gd_wk = GDN_HEADS * GDN_DK
    gd_wv = GDN_HEADS * GDN_DV
    hg_w = HG_HEADS * HG_DK

    mod = _ada(c, ada_w, ada_b)
    xf = x.reshape(bsz * seq, d)
    for l in range(depth):
        mod_l = mod[l].reshape(bsz, N_MOD, d)
        xf = _ffn(xf, mod_l, ffn1_norm[l], ffn1_w_in[l].astype(BF16), ffn1_w_out[l].astype(BF16),
                  final_norm, seq=seq, base=0, final=False)

        wi = mix_w_in[l].astype(BF16)
        o_gd = 3 * sb_w
        o_b = o_gd + 3 * gd_wk + gd_wv
        o_hg = o_b + 2 * GDN_HEADS
        w_ba = jnp.pad(wi[:, o_b:o_hg], ((0, 0), (0, LANES - 2 * GDN_HEADS)))
        sb, gd, hg, ba = _inproj(xf, mod_l, mix_norm[l], wi[:, :o_gd], wi[:, o_gd:o_b], wi[:, o_hg:], w_ba,
                                 seq=seq, base=3)

        o_sb = _sb_attention(sb, sb_out_norm[l], bsz=bsz, seq=seq)
        o_gdn = _gdn(gd, ba, gdn_conv_w[l], gdn_A_log[l], gdn_dt_bias[l], gdn_out_norm[l], bsz=bsz, seq=seq)
        o_hgr = _hgrn2(hg, hg_lb_logits, hg_out_norm[l], bsz=bsz, seq=seq, layer=l)

        wo = mix_w_out[l].astype(BF16)
        xf = _outproj(xf, mod_l, o_sb, o_gdn, o_hgr, wo[:sb_w], wo[sb_w:sb_w + gd_wv], wo[sb_w + gd_wv:],
                      seq=seq, base=5)

        xf = _ffn(xf, mod_l, ffn2_norm[l], ffn2_w_in[l].astype(BF16), ffn2_w_out[l].astype(BF16),
                  final_norm, seq=seq, base=6, final=(l == depth - 1))
    return xf.reshape(bsz, seq, d)
```

```python
import functools

import jax
import jax.numpy as jnp
from jax import lax
from jax.experimental import pallas as pl
from jax.experimental.pallas import tpu as pltpu

F32 = jnp.float32
BF16 = jnp.bfloat16

EPS = 1e-6
HALF_STEP = 0.5
N_MOD = 9

SB_HEADS, SB_DIM = 4, 64
LOG2E = 1.4426950408889634
SB_Q_SCALE = SB_DIM ** -0.5 * LOG2E
GDN_HEADS, GDN_DK, GDN_DV, GDN_CONV, GDN_CHUNK = 4, 128, 128, 4, 64
HG_HEADS, HG_DK, HG_DV, HG_CHUNK = 4, 64, 64, 64

LANES = 128
SUBLANES = 8
VMEM_LIMIT = 56 * 1024 * 1024

_NT = (((1,), (1,)), ((), ()))
_TN = (((0,), (0,)), ((), ()))


def _dot(a, b):
    return jnp.dot(a, b, preferred_element_type=F32)


def _dot_nt(a, b):
    return lax.dot_general(a, b, _NT, preferred_element_type=F32)


def _dot_tn(a, b):
    return lax.dot_general(a, b, _TN, preferred_element_type=F32)


def _split(x, parts):
    out = []
    r = x
    for _ in range(parts):
        p = r.astype(BF16)
        out.append(p)
        r = r - p.astype(F32)
    return out


def _dot_exact_rhs(x, m, parts):
    acc = None
    for p in _split(x, parts):
        t = _dot(p, m)
        acc = t if acc is None else acc + t
    return acc


def _dot_exact_lhs(m, x, parts):
    acc = None
    for p in _split(x, parts):
        t = _dot(m, p)
        acc = t if acc is None else acc + t
    return acc


def _silu(x):
    return x * jax.nn.sigmoid(x)


def _softplus(x):
    return jnp.maximum(x, 0.0) + jnp.log1p(jnp.exp(-jnp.abs(x)))


def _iota(shape, dim):
    return lax.broadcasted_iota(jnp.int32, shape, dim)


def _params(sem):
    return pltpu.CompilerParams(dimension_semantics=sem, vmem_limit_bytes=VMEM_LIMIT)


def _ada_kernel(c_ref, w_ref, b_ref, o_ref):
    ca = _silu(c_ref[...]).astype(BF16)
    o_ref[0] = _dot(ca, w_ref[0].astype(BF16)) + b_ref[0]


def _ada(c, ada_w, ada_b):
    depth, d, n = ada_w.shape
    bsz = c.shape[0]
    tn = 1152
    assert n % tn == 0
    return pl.pallas_call(
        _ada_kernel,
        grid=(depth, n // tn),
        in_specs=[
            pl.BlockSpec((bsz, d), lambda l, j: (0, 0)),
            pl.BlockSpec((1, d, tn), lambda l, j: (l, 0, j)),
            pl.BlockSpec((1, 1, tn), lambda l, j: (l, 0, j)),
        ],
        out_specs=pl.BlockSpec((1, bsz, tn), lambda l, j: (l, 0, j)),
        out_shape=jax.ShapeDtypeStruct((depth, bsz, n), F32),
        compiler_params=_params(("arbitrary", "arbitrary")),
        name="ada_mod",
    )(c, ada_w, ada_b.reshape(depth, 1, n))


def _norm_mod(x, gain, shift, scale):
    y = x * lax.rsqrt(jnp.mean(x * x, axis=-1, keepdims=True) + EPS)
    return (y * gain) * (1.0 + scale) + shift


MXU_WIDTH = 256


def _ff_chunks(d_ff, target):
    assert d_ff % MXU_WIDTH == 0 and target % MXU_WIDTH == 0
    out, c0 = [], 0
    while c0 < d_ff:
        cn = min(target, d_ff - c0)
        out.append((c0, cn))
        c0 += cn
    return tuple(out)


def _ffn_kernel(x_ref, mod_ref, g_ref, wi_ref, wo_ref, fg_ref, o_ref, *, base, final, chunks):
    d_ff = wo_ref.shape[0]
    m = mod_ref[0]
    x = x_ref[...]
    h = _norm_mod(x, g_ref[...], m[base:base + 1], m[base + 1:base + 2]).astype(BF16)
    acc = None
    for c0, cn in chunks:
        gate = _dot(h, wi_ref[:, c0:c0 + cn])
        up = _dot(h, wi_ref[:, d_ff + c0:d_ff + c0 + cn])
        t = _dot((_silu(gate) * up).astype(BF16), wo_ref[c0:c0 + cn, :])
        acc = t if acc is None else acc + t
    out = x + (HALF_STEP * m[base + 2:base + 3]) * acc
    if final:
        out = (out * lax.rsqrt(jnp.mean(out * out, axis=-1, keepdims=True) + EPS)) * fg_ref[...]
    o_ref[...] = out


def _ffn(xf, mod_l, gain, w_in, w_out, final_gain, *, seq, base, final):
    bt, d = xf.shape
    d_ff = w_out.shape[0]
    tm = 512
    assert bt % tm == 0 and seq % tm == 0
    resident = lambda a: pl.BlockSpec(a.shape, lambda i: (0, 0), pipeline_mode=pl.Buffered(1))
    return pl.pallas_call(
        functools.partial(_ffn_kernel, base=base, final=final, chunks=_ff_chunks(d_ff, 1024)),
        grid=(bt // tm,),
        in_specs=[
            pl.BlockSpec((tm, d), lambda i: (i, 0)),
            pl.BlockSpec((1, N_MOD, d), lambda i: ((i * tm) // seq, 0, 0)),
            pl.BlockSpec((1, d), lambda i: (0, 0)),
            resident(w_in),
            resident(w_out),
            pl.BlockSpec((1, d), lambda i: (0, 0)),
        ],
        out_specs=pl.BlockSpec((tm, d), lambda i: (i, 0)),
        out_shape=jax.ShapeDtypeStruct((bt, d), F32),
        compiler_params=_params(("arbitrary",)),
        name="ffn_half_step",
    )(xf, mod_l, gain.reshape(1, d), w_in, w_out, final_gain.reshape(1, d))


HALO = SUBLANES


def _inproj_kernel(x_ref, mod_ref, g_ref, wsb_ref, wgd_ref, whg_ref, wba_ref, cw_ref, par_ref,
                   sb_ref, gd_ref, hg_ref, ba_ref, ext_sc, *, base, tiles_per_seq):
    tm = x_ref.shape[0]
    wq = GDN_HEADS * GDN_DK
    nqkv = 3 * wq

    @pl.when(pl.program_id(0) % tiles_per_seq == 0)
    def _():
        ext_sc[0:HALO, :] = jnp.zeros((HALO, nqkv), F32)

    m = mod_ref[0]
    h = _norm_mod(x_ref[...], g_ref[...], m[base:base + 1], m[base + 1:base + 2]).astype(BF16)

    def conv_part(part, g):
        c0 = part * wq
        ext_sc[HALO:HALO + tm, c0:c0 + wq] = g
        for hd in range(GDN_HEADS):
            cols = slice(c0 + hd * GDN_DK, c0 + (hd + 1) * GDN_DK)
            y = cw_ref[GDN_CONV - 1:GDN_CONV, cols] * g[:, hd * GDN_DK:(hd + 1) * GDN_DK]
            for i in range(1, GDN_CONV):
                y = y + cw_ref[GDN_CONV - 1 - i:GDN_CONV - i, cols] * ext_sc[HALO - i:HALO - i + tm, cols]
            y = _silu(y)
            if part < 2:
                y = y * lax.rsqrt(jnp.sum(y * y, axis=-1, keepdims=True) + EPS)
                if part == 0:
                    y = y * (GDN_DK ** -0.5)
            gd_ref[:, cols] = y
        ext_sc[0:HALO, c0:c0 + wq] = ext_sc[tm:tm + HALO, c0:c0 + wq]

    nq = SB_HEADS * SB_DIM
    g_q = _dot(h, wgd_ref[:, 0:wq])
    g_k = _dot(h, wgd_ref[:, wq:2 * wq])
    conv_part(0, g_q)
    g_v = _dot(h, wgd_ref[:, 2 * wq:nqkv])
    conv_part(1, g_k)
    gd_ref[:, nqkv:] = _dot(h, wgd_ref[:, nqkv:])
    sb = _dot(h, wsb_ref[...])
    conv_part(2, g_v)
    sb = jnp.where(_iota(sb.shape, 1) < nq, sb * SB_Q_SCALE, sb)
    sb_ref[...] = sb.astype(BF16)
    hg_ref[...] = _dot(h, whg_ref[...])

    ba = _dot(h, wba_ref[...])
    log_a = -jnp.exp(par_ref[0:1, :]) * _softplus(ba + par_ref[1:2, :])
    ba_ref[...] = jnp.where(_iota(ba.shape, 1) < GDN_HEADS, jax.nn.sigmoid(ba), log_a)


def _inproj(xf, mod_l, gain, w_sb, w_gd, w_hg, w_ba, conv_w, a_log, dt_bias, *, seq, base):
    bt, d = xf.shape
    tm = 512
    assert bt % tm == 0 and seq % tm == 0
    n_sb, n_gd, n_hg, n_ba = w_sb.shape[1], w_gd.shape[1], w_hg.shape[1], w_ba.shape[1]
    par = jnp.zeros((SUBLANES, LANES), F32)
    par = par.at[0, GDN_HEADS:2 * GDN_HEADS].set(a_log).at[1, GDN_HEADS:2 * GDN_HEADS].set(dt_bias)
    row = lambda n: pl.BlockSpec((tm, n), lambda i: (i, 0))
    full = lambda a: pl.BlockSpec(a.shape, lambda i: (0, 0))
    return pl.pallas_call(
        functools.partial(_inproj_kernel, base=base, tiles_per_seq=seq // tm),
        grid=(bt // tm,),
        in_specs=[
            row(d),
            pl.BlockSpec((1, N_MOD, d), lambda i: ((i * tm) // seq, 0, 0)),
            pl.BlockSpec((1, d), lambda i: (0, 0)),
            full(w_sb), full(w_gd), full(w_hg), full(w_ba), full(conv_w), full(par),
        ],
        out_specs=[row(n_sb), row(n_gd), row(n_hg), row(n_ba)],
        out_shape=[
            jax.ShapeDtypeStruct((bt, n_sb), BF16),
            jax.ShapeDtypeStruct((bt, n_gd), F32),
            jax.ShapeDtypeStruct((bt, n_hg), F32),
            jax.ShapeDtypeStruct((bt, n_ba), F32),
        ],
        scratch_shapes=[pltpu.VMEM((tm + HALO, 3 * GDN_HEADS * GDN_DK), F32)],
        compiler_params=_params(("arbitrary",)),
        name="mixer_in_proj",
    )(xf, mod_l, gain.reshape(1, d), w_sb, w_gd, w_hg, w_ba, conv_w, par)


SB_TQ = 512
SB_TC = 256


def _sb_kernel(q_ref, k_ref, v_ref, gain_ref, o_ref, *, seq):
    tq, tc = SB_TQ, SB_TC
    lane = _iota((1, LANES), 1)
    u = jnp.where(_iota((tc, tc), 0) >= _iota((tc, tc), 1), 1.0, 0.0).astype(BF16)
    hh = _iota((LANES, LANES), 0) // SB_DIM
    hc = _iota((LANES, LANES), 1) // SB_DIM
    bd = jnp.where(hh == hc, 1.0, 0.0).astype(BF16)
    causal = _iota((tc, tc), 1) < _iota((tc, tc), 0)

    def step(qs, k0s, nk, carries, accs, mask):
        ks = [k_ref[pl.ds(k0, nk), :] for k0 in k0s]
        vs = [v_ref[pl.ds(k0, nk), :] for k0 in k0s]
        zs = [_dot_nt(q, k) for q, k in zip(qs, ks)]
        sps = [jnp.maximum(z, 0.0) + jnp.log2(1.0 + jnp.exp2(-jnp.abs(z))) for z in zs]
        if mask is not None:
            sps = [jnp.where(mask, sp, 0.0) for sp in sps]
        ws = [[None] * (nk // tc) for _ in qs]
        for g in reversed(range(nk // tc)):
            cols = slice(g * tc, (g + 1) * tc)
            rs = [_dot_exact_rhs(sp[:, cols], u, 2) + c for sp, c in zip(sps, carries)]
            for i, (z, r) in enumerate(zip(zs, rs)):
                ws[i][g] = jnp.exp2(z[:, cols] - r)
            carries = [c + jnp.sum(sp[:, cols], axis=-1, keepdims=True) for sp, c in zip(sps, carries)]
        ws = [w[0] if len(w) == 1 else jnp.concatenate(w, axis=1) for w in ws]
        if mask is not None:
            ws = [jnp.where(mask, w, 0.0) for w in ws]
        accs = [a + _dot(w.astype(BF16), v) for a, w, v in zip(accs, ws, vs)]
        return carries, accs

    nhp = LANES // SB_DIM
    nparts = tq // tc

    def qblock(qi, _):
        q0 = pl.multiple_of(qi * tq, tq)
        q2 = q_ref[pl.ds(q0, tq), :]
        hms = [(lane // SB_DIM) == h for h in range(nhp)]
        qhs = [jnp.where(hm, q2, jnp.zeros_like(q2)) for hm in hms]
        qps = [qh[p * tc:(p + 1) * tc] for qh in qhs for p in range(nparts)]
        k0s = [pl.multiple_of(q0 + p * tc, tc) for _ in qhs for p in range(nparts)]
        cs = [jnp.zeros((tc, 1), F32) for _ in qps]
        accs = [jnp.zeros((tc, LANES), F32) for _ in qps]
        cs, accs = step(qps, k0s, tc, cs, accs, causal)
        for back in range(1, nparts):
            sel = [i for i in range(len(qps)) if i % nparts >= back]
            c2, a2 = step([qps[i] for i in sel],
                          [pl.multiple_of(q0 + (i % nparts - back) * tc, tc) for i in sel], tc,
                          [cs[i] for i in sel], [accs[i] for i in sel], None)
            for j, i in enumerate(sel):
                cs[i], accs[i] = c2[j], a2[j]
        carries = [jnp.concatenate(cs[h * nparts:(h + 1) * nparts], axis=0) for h in range(nhp)]
        accs = [jnp.concatenate(accs[h * nparts:(h + 1) * nparts], axis=0) for h in range(nhp)]

        out = jnp.zeros((tq, LANES), F32)
        for h in range(nhp):
            def body(i, ca, qh=qhs[h]):
                k0 = pl.multiple_of((qi - 1 - i) * tq, tq)
                c2, a2 = step([qh], [k0], tq, [ca[0]], [ca[1]], None)
                return c2[0], a2[0]

            _, acc = lax.fori_loop(0, qi, body, (carries[h], accs[h]))
            out = out + jnp.where(hms[h], acc, 0.0)
        ms = _dot_exact_rhs(out * out, bd, 2) * (1.0 / SB_DIM)
        y = (out * lax.rsqrt(ms + EPS)) * gain_ref[...]
        o_ref[pl.ds(q0, tq), :] = y.astype(BF16)
        return 0

    lax.fori_loop(0, seq // tq, qblock, 0)


def _sb_attention(sb, gain, *, bsz, seq):
    npair = SB_HEADS * SB_DIM // LANES
    gain2 = jnp.tile(gain.reshape(1, SB_DIM), (1, LANES // SB_DIM))
    blk = lambda off: pl.BlockSpec((seq, LANES), lambda b, p: (b, off + p))
    return pl.pallas_call(
        functools.partial(_sb_kernel, seq=seq),
        grid=(bsz, npair),
        in_specs=[blk(0), blk(npair), blk(2 * npair), pl.BlockSpec((1, LANES), lambda b, p: (0, 0))],
        out_specs=pl.BlockSpec((seq, LANES), lambda b, p: (b, p)),
        out_shape=jax.ShapeDtypeStruct((bsz * seq, npair * LANES), BF16),
        compiler_params=_params(("arbitrary", "arbitrary")),
        name="sb_attention",
    )(sb, sb, sb, gain2)


GDN_TT = 256


def _gdn_kernel(gd_ref, ba_ref, gain_ref, o_ref, bl_sc, s_sc):
    tt = GDN_TT
    c = GDN_CHUNK
    nh = GDN_HEADS
    dk = GDN_DK
    wq = nh * dk
    nqkv = 3 * wq
    hc = nh * c
    qkv_sc = gd_ref

    @pl.when(pl.program_id(1) == 0)
    def _():
        s_sc[...] = jnp.zeros_like(s_sc)

    ba = ba_ref[...]
    lane = _iota((1, LANES), 1)
    for j in range(2 * nh):
        col = jnp.sum(jnp.where(lane == j, ba, 0.0), axis=-1, keepdims=True)
        bl_sc[j] = jnp.broadcast_to(col, (tt, LANES))

    r = _iota((hc, hc), 0)
    s = _iota((hc, hc), 1)
    same_head = (r // c) == (s // c)
    incl = same_head & (s <= r)
    strict = same_head & (s < r)
    tri = jnp.where(incl, 1.0, 0.0).astype(BF16)
    eye = jnp.where(r == s, 1.0, 0.0)

    rxs = r ^ s
    lmask = {m: jnp.where((s < r) & (rxs >= m) & (rxs < 2 * m), 1.0, 0.0).astype(BF16)
             for m in (1, 2, 4, 8, 16, 32)}
    assert 2 * max(lmask) == c

    nch = tt // c
    rows_of = [pl.ds(ci * c, c) for ci in range(nch)]

    def stack(rows, off):
        return jnp.concatenate([qkv_sc[rows, off + h * dk:off + (h + 1) * dk] for h in range(nh)], axis=0)

    q_st = [stack(rw, 0) for rw in rows_of]
    k_st = [stack(rw, wq) for rw in rows_of]
    v_st = [stack(rw, 2 * wq) for rw in rows_of]
    beta_st = [jnp.concatenate([bl_sc[h, rw, :] for h in range(nh)], axis=0) for rw in rows_of]
    la_bc = [jnp.concatenate([bl_sc[nh + h, rw, :] for h in range(nh)], axis=0) for rw in rows_of]
    g_bc = [_dot_exact_lhs(tri, la, 3) for la in la_bc]
    g_last = [jnp.concatenate(
        [jnp.broadcast_to(g[h * c + c - 1:h * c + c, :], (c, LANES)) for h in range(nh)], axis=0)
        for g in g_bc]
    ex = [jnp.exp(jnp.concatenate([g, g], axis=1) - jnp.broadcast_to(g.T[0:1, :], (hc, hc))) for g in g_bc]
    kb_st = [k * b for k, b in zip(k_st, beta_st)]
    k_bf = [k.astype(BF16) for k in k_st]
    a_kk = [(_dot_nt(kb.astype(BF16), kf) * jnp.where(strict, e, 0.0)).astype(BF16)
            for kb, kf, e in zip(kb_st, k_bf, ex)]
    a_qk = [(_dot_nt(q.astype(BF16), kf) * jnp.where(incl, e, 0.0)).astype(BF16)
            for q, kf, e in zip(q_st, k_bf, ex)]

    xs = [eye - (a * lmask[1]).astype(F32) for a in a_kk]
    m = 2
    while m < c:
        xb = [x.astype(BF16) for x in xs]
        ys = [_dot(a * lmask[m], b).astype(BF16) for a, b in zip(a_kk, xb)]
        xs = [x - _dot(b, y) for x, b, y in zip(xs, xb, ys)]
        m *= 2

    eg = [jnp.exp(g) for g in g_bc]
    uw = [_dot(x.astype(BF16), jnp.concatenate([v * b, kb * e], axis=1).astype(BF16))
          for x, v, b, kb, e in zip(xs, v_st, beta_st, kb_st, eg)]
    q_decs = [(q * e).astype(BF16) for q, e in zip(q_st, eg)]
    k_decs = [(k * jnp.exp(gl - g)).astype(BF16) for k, gl, g in zip(k_st, g_last, g_bc)]
    cds = [jnp.exp(gl) for gl in g_last]

    hs = [slice(h * c, (h + 1) * c) for h in range(nh)]
    u_bf = [x[:, :dk].astype(BF16) for x in uw]
    w_bf = [x[:, dk:].astype(BF16) for x in uw]
    ktu = [[_dot_tn(kd[sl], ub[sl]) for sl in hs] for kd, ub in zip(k_decs, u_bf)]
    ktw = [[_dot_tn(kd[sl], wb[sl]).astype(BF16) for sl in hs] for kd, wb in zip(k_decs, w_bf)]
    s_cur = [s_sc[h] for h in range(nh)]
    s_in = []
    for ci in range(nch):
        s_bf = [s.astype(BF16) for s in s_cur]
        s_in.append(s_bf)
        s_cur = [s * cds[ci][h * c:h * c + 1, :] + ktu[ci][h] - _dot(ktw[ci][h], s_bf[h])
                 for h, s in enumerate(s_cur)]
    for h in range(nh):
        s_sc[h] = s_cur[h]

    for ci in range(nch):
        rows = rows_of[ci]
        v_new_st = jnp.concatenate(
            [uw[ci][sl, :dk] - _dot(w_bf[ci][sl], s_in[ci][h]) for h, sl in enumerate(hs)], axis=0)
        o_st = jnp.concatenate([_dot(q_decs[ci][sl], s_in[ci][h]) for h, sl in enumerate(hs)], axis=0)
        o_st = o_st + _dot(a_qk[ci], v_new_st.astype(BF16))
        o_st = (o_st * lax.rsqrt(jnp.mean(o_st * o_st, axis=-1, keepdims=True) + EPS)) * gain_ref[...]
        o_t = jnp.concatenate([o_st[h * c:(h + 1) * c] for h in range(nh)], axis=1)
        z = gd_ref[rows, nqkv:nqkv + wq]
        o_ref[rows, :] = (o_t * _silu(z)).astype(BF16)


def _gdn(gd, ba, gain, *, bsz, seq):
    tt = GDN_TT
    nt = seq // tt
    wq = GDN_HEADS * GDN_DK
    return pl.pallas_call(
        _gdn_kernel,
        grid=(bsz, nt),
        in_specs=[
            pl.BlockSpec((tt, gd.shape[1]), lambda b, t: (b * nt + t, 0)),
            pl.BlockSpec((tt, LANES), lambda b, t: (b * nt + t, 0)),
            pl.BlockSpec((1, GDN_DV), lambda b, t: (0, 0)),
        ],
        out_specs=pl.BlockSpec((tt, wq), lambda b, t: (b * nt + t, 0)),
        out_shape=jax.ShapeDtypeStruct((bsz * seq, wq), BF16),
        scratch_shapes=[
            pltpu.VMEM((2 * GDN_HEADS, tt, LANES), F32),
            pltpu.VMEM((GDN_HEADS, GDN_DK, GDN_DV), F32),
        ],
        compiler_params=_params(("arbitrary", "arbitrary")),
        name="gated_deltanet",
    )(gd, ba, gain.reshape(1, GDN_DV))


HG_TT = 256


def _hg_kernel(hg_ref, lbl_ref, gain_ref, o_ref, st_sc, *, layer):
    tt = HG_TT
    c = HG_CHUNK
    w = HG_HEADS * HG_DK
    ti = pl.program_id(1)

    @pl.when(ti == 0)
    def _():
        st_sc[...] = jnp.zeros_like(st_sc)

    lg = lbl_ref[...]
    e = jnp.exp(lg - jnp.max(lg, axis=0, keepdims=True))
    p = e / jnp.sum(e, axis=0, keepdims=True)
    lb = jnp.zeros((1, w), F32)
    for i in range(1, layer + 1):
        lb = lb + p[i:i + 1, :]
    log_lb = jnp.log(lb)
    log1m_lb = jnp.log1p(-lb)

    hr = _iota((w, w), 0) // HG_DK
    hcn = _iota((w, w), 1) // HG_DK
    same_head = hr == hcn
    bd = jnp.where(same_head, 1.0, 0.0).astype(BF16)
    tr = _iota((c, c), 0)
    ts = _iota((c, c), 1)
    tri = jnp.where(ts <= tr, 1.0, 0.0).astype(BF16)
    levels = (SUBLANES, 2 * SUBLANES, 4 * SUBLANES)
    assert 2 * levels[-1] == c
    trow = _iota((c, w), 0)
    second_half = {m: ((trow // m) % 2) == 1 for m in levels}
    lane_head = _iota((1, w), 1) // HG_DK
    head_rows = (_iota((HG_HEADS * c, w), 0) // c) == (_iota((HG_HEADS * c, w), 1) // HG_DK)
    st_t = _iota((HG_HEADS * c, c), 0) % c
    st_s = _iota((HG_HEADS * c, c), 1)
    same_block = {m: (st_t // (2 * m)) == (st_s // (2 * m)) for m in levels}

    nch = tt // c
    rows_of = [pl.ds(ci * c, c) for ci in range(nch)]
    q = [_silu(hg_ref[rw, 0:w]) for rw in rows_of]
    f_pre = [hg_ref[rw, w:2 * w] for rw in rows_of]
    v = [hg_ref[rw, 2 * w:3 * w] for rw in rows_of]
    v_bf = [x.astype(BF16) for x in v]
    log_sig = [jnp.minimum(f, 0.0) - jnp.log1p(jnp.exp(-jnp.abs(f))) for f in f_pre]
    bterm = [log1m_lb + x for x in log_sig]
    lf = [jnp.maximum(log_lb, x) + jnp.log1p(jnp.exp(-jnp.abs(log_lb - x))) for x in bterm]
    k_in = [(1.0 - lb) * jax.nn.sigmoid(-f) for f in f_pre]
    b = [_dot_exact_lhs(tri, x, 3) for x in lf]
    b_last = [x[c - 1:c, :] for x in b]

    k_dec = [(k * jnp.exp(bl - x)).astype(BF16) for k, bl, x in zip(k_in, b_last, b)]
    upd = [jnp.where(same_head, _dot_tn(vb, kd), 0.0) for vb, kd in zip(v_bf, k_dec)]
    dec = [jnp.exp(bl) for bl in b_last]
    st = st_sc[...]
    st_in = []
    for ci in range(nch):
        st_in.append(st.astype(BF16))
        st = st * dec[ci] + upd[ci]
    st_sc[...] = st
    o = [_dot_nt((qq * jnp.exp(x)).astype(BF16), s) for qq, x, s in zip(q, b, st_in)]

    a_st = [None] * nch
    for m in levels:
        br = [jnp.concatenate(
            [jnp.broadcast_to(x[g0 + m:g0 + m + 1, :], (2 * m, w)) for g0 in range(0, c, 2 * m)], axis=0)
            for x in b]
        e = [jnp.exp(-jnp.abs(x - r)) for x, r in zip(b, br)]
        second = second_half[m]
        qe = [jnp.where(second, qq * ee, 0.0) for qq, ee in zip(q, e)]
        ke = [jnp.where(second, 0.0, kk * ee).astype(BF16) for kk, ee in zip(k_in, e)]
        q_stk = [jnp.where(head_rows, jnp.concatenate([x] * HG_HEADS, axis=0), 0.0).astype(BF16) for x in qe]
        am = [_dot_nt(qs, kk) for qs, kk in zip(q_stk, ke)]
        if 2 * m < c:
            am = [jnp.where(same_block[m], x, 0.0) for x in am]
        a_st = [x if a is None else a + x for a, x in zip(a_st, am)]
    o_stk = [_dot(a.astype(BF16), vb) for a, vb in zip(a_st, v_bf)]
    for h in range(HG_HEADS):
        o = [oo + jnp.where(lane_head == h, os[h * c:(h + 1) * c], 0.0) for oo, os in zip(o, o_stk)]

    nt8 = c // SUBLANES
    sub = _iota((SUBLANES, 1), 0)
    ws = [[] for _ in range(nch)]
    for g in range(nt8):
        t0 = g * SUBLANES
        for i in range(SUBLANES):
            ok = sub >= i
            for ci in range(nch):
                qt, bt = q[ci][t0:t0 + SUBLANES, :], b[ci][t0:t0 + SUBLANES, :]
                ex = jnp.where(ok, jnp.exp(jnp.where(ok, bt - b[ci][t0 + i:t0 + i + 1, :], 0.0)), 0.0)
                ws[ci].append((qt * ex) * k_in[ci][t0 + i:t0 + i + 1, :])
    cw = [_dot(jnp.concatenate(x, axis=0).astype(BF16), bd) for x in ws]
    for ci in range(nch):
        o_tiles = []
        for g in range(nt8):
            acc = None
            for i in range(SUBLANES):
                s = g * SUBLANES + i
                t = cw[ci][s * SUBLANES:(s + 1) * SUBLANES] * v[ci][s:s + 1, :]
                acc = t if acc is None else acc + t
            o_tiles.append(acc)
        o[ci] = o[ci] + jnp.concatenate(o_tiles, axis=0)
    ms = [_dot_exact_rhs(x * x, bd, 2) * (1.0 / HG_DV) for x in o]
    for ci in range(nch):
        y = (o[ci] * lax.rsqrt(ms[ci] + EPS)) * gain_ref[...]
        o_ref[rows_of[ci], :] = (y * _silu(hg_ref[rows_of[ci], 3 * w:4 * w])).astype(BF16)


def _hgrn2(hg, lb_logits, gain, *, bsz, seq, layer):
    tt = HG_TT
    nt = seq // tt
    w = HG_HEADS * HG_DV
    gain4 = jnp.tile(gain.reshape(1, HG_DV), (1, HG_HEADS))
    return pl.pallas_call(
        functools.partial(_hg_kernel, layer=layer),
        grid=(bsz, nt),
        in_specs=[
            pl.BlockSpec((tt, hg.shape[1]), lambda b, t: (b * nt + t, 0)),
            pl.BlockSpec(lb_logits.shape, lambda b, t: (0, 0)),
            pl.BlockSpec((1, w), lambda b, t: (0, 0)),
        ],
        out_specs=pl.BlockSpec((tt, w), lambda b, t: (b * nt + t, 0)),
        out_shape=jax.ShapeDtypeStruct((bsz * seq, w), BF16),
        scratch_shapes=[pltpu.VMEM((w, w), F32)],
        compiler_params=_params(("arbitrary", "arbitrary")),
        name="hgrn2",
    )(hg, lb_logits, gain4)


def _outproj_kernel(x_ref, mod_ref, osb_ref, ogd_ref, ohg_ref, wsb_ref, wgd_ref, whg_ref, o_ref, *, base):
    y = _dot(osb_ref[...], wsb_ref[...]) + _dot(ogd_ref[...], wgd_ref[...]) + _dot(ohg_ref[...], whg_ref[...])
    m = mod_ref[0]
    o_ref[...] = x_ref[...] + m[base:base + 1] * y


def _outproj(xf, mod_l, o_sb, o_gd, o_hg, w_sb, w_gd, w_hg, *, seq, base):
    bt, d = xf.shape
    tm = 512
    row = lambda n: pl.BlockSpec((tm, n), lambda i: (i, 0))
    full = lambda a: pl.BlockSpec(a.shape, lambda i: (0, 0))
    return pl.pallas_call(
        functools.partial(_outproj_kernel, base=base),
        grid=(bt // tm,),
        in_specs=[
            row(d),
            pl.BlockSpec((1, N_MOD, d), lambda i: ((i * tm) // seq, 0, 0)),
            row(o_sb.shape[1]), row(o_gd.shape[1]), row(o_hg.shape[1]),
            full(w_sb), full(w_gd), full(w_hg),
        ],
        out_specs=row(d),
        out_shape=jax.ShapeDtypeStruct((bt, d), F32),
        compiler_params=_params(("arbitrary",)),
        name="mixer_out_proj",
    )(xf, mod_l, o_sb, o_gd, o_hg, w_sb, w_gd, w_hg)


def kernel(x, c, ffn1_norm, ffn1_w_in, ffn1_w_out, mix_norm, mix_w_in, gdn_conv_w, gdn_A_log, gdn_dt_bias,
           sb_out_norm, gdn_out_norm, hg_out_norm, hg_lb_logits, mix_w_out, ffn2_norm, ffn2_w_in, ffn2_w_out,
           ada_w, ada_b, final_norm):
    bsz, seq, d = x.shape
    depth = ada_w.shape[0]
    sb_w = SB_HEADS * SB_DIM
    gd_wk = GDN_HEADS * GDN_DK
    gd_wv = GDN_HEADS * GDN_DV
    hg_w = HG_HEADS * HG_DK

    mod = _ada(c, ada_w, ada_b)
    xf = x.reshape(bsz * seq, d)
    for l in range(depth):
        mod_l = mod[l].reshape(bsz, N_MOD, d)
        xf = _ffn(xf, mod_l, ffn1_norm[l], ffn1_w_in[l].astype(BF16), ffn1_w_out[l].astype(BF16),
                  final_norm, seq=seq, base=0, final=False)

        wi = mix_w_in[l].astype(BF16)
        o_gd = 3 * sb_w
        o_b = o_gd + 3 * gd_wk + gd_wv
        o_hg = o_b + 2 * GDN_HEADS
        w_ba = jnp.pad(wi[:, o_b:o_hg], ((0, 0), (0, LANES - 2 * GDN_HEADS)))
        sb, gd, hg, ba = _inproj(xf, mod_l, mix_norm[l], wi[:, :o_gd], wi[:, o_gd:o_b], wi[:, o_hg:], w_ba,
                                 gdn_conv_w[l], gdn_A_log[l], gdn_dt_bias[l], seq=seq, base=3)

        o_sb = _sb_attention(sb, sb_out_norm[l], bsz=bsz, seq=seq)
        o_gdn = _gdn(gd, ba, gdn_out_norm[l], bsz=bsz, seq=seq)
        o_hgr = _hgrn2(hg, hg_lb_logits, hg_out_norm[l], bsz=bsz, seq=seq, layer=l)

        wo = mix_w_out[l].astype(BF16)
        xf = _outproj(xf, mod_l, o_sb, o_gdn, o_hgr, wo[:sb_w], wo[sb_w:sb_w + gd_wv], wo[sb_w + gd_wv:],
                      seq=seq, base=5)

        xf = _ffn(xf, mod_l, ffn2_norm[l], ffn2_w_in[l].astype(BF16), ffn2_w_out[l].astype(BF16),
                  final_norm, seq=seq, base=6, final=(l == depth - 1))
    return xf.reshape(bsz, seq, d)
```

```python
import functools

import jax
import jax.numpy as jnp
from jax import lax
from jax.experimental import pallas as pl
from jax.experimental.pallas import tpu as pltpu

F32 = jnp.float32
BF16 = jnp.bfloat16

EPS = 1e-6
HALF_STEP = 0.5
N_MOD = 9

SB_HEADS, SB_DIM = 4, 64
LOG2E = 1.4426950408889634
SB_Q_SCALE = SB_DIM ** -0.5 * LOG2E
GDN_HEADS, GDN_DK, GDN_DV, GDN_CONV, GDN_CHUNK = 4, 128, 128, 4, 64
HG_HEADS, HG_DK, HG_DV, HG_CHUNK = 4, 64, 64, 64

LANES = 128
SUBLANES = 8
VMEM_LIMIT = 56 * 1024 * 1024

_NT = (((1,), (1,)), ((), ()))
_TN = (((0,), (0,)), ((), ()))


def _dot(a, b):
    return jnp.dot(a, b, preferred_element_type=F32)


def _dot_nt(a, b):
    return lax.dot_general(a, b, _NT, preferred_element_type=F32)


def _dot_tn(a, b):
    return lax.dot_general(a, b, _TN, preferred_element_type=F32)


def _split(x, parts):
    out = []
    r = x
    for _ in range(parts):
        p = r.astype(BF16)
        out.append(p)
        r = r - p.astype(F32)
    return out


def _dot_exact_rhs(x, m, parts):
    acc = None
    for p in _split(x, parts):
        t = _dot(p, m)
        acc = t if acc is None else acc + t
    return acc


def _dot_exact_lhs(m, x, parts):
    acc = None
    for p in _split(x, parts):
        t = _dot(m, p)
        acc = t if acc is None else acc + t
    return acc


def _silu(x):
    return x * jax.nn.sigmoid(x)


def _softplus(x):
    return jnp.maximum(x, 0.0) + jnp.log1p(jnp.exp(-jnp.abs(x)))


def _iota(shape, dim):
    return lax.broadcasted_iota(jnp.int32, shape, dim)


def _params(sem):
    return pltpu.CompilerParams(dimension_semantics=sem, vmem_limit_bytes=VMEM_LIMIT)


def _ada_kernel(c_ref, w_ref, b_ref, o_ref):
    ca = _silu(c_ref[...]).astype(BF16)
    o_ref[0] = _dot(ca, w_ref[0].astype(BF16)) + b_ref[0]


def _ada(c, ada_w, ada_b):
    depth, d, n = ada_w.shape
    bsz = c.shape[0]
    tn = 1152
    assert n % tn == 0
    return pl.pallas_call(
        _ada_kernel,
        grid=(depth, n // tn),
        in_specs=[
            pl.BlockSpec((bsz, d), lambda l, j: (0, 0)),
            pl.BlockSpec((1, d, tn), lambda l, j: (l, 0, j)),
            pl.BlockSpec((1, 1, tn), lambda l, j: (l, 0, j)),
        ],
        out_specs=pl.BlockSpec((1, bsz, tn), lambda l, j: (l, 0, j)),
        out_shape=jax.ShapeDtypeStruct((depth, bsz, n), F32),
        compiler_params=_params(("arbitrary", "arbitrary")),
        name="ada_mod",
    )(c, ada_w, ada_b.reshape(depth, 1, n))


def _norm_mod(x, gain, shift, scale):
    y = x * lax.rsqrt(jnp.mean(x * x, axis=-1, keepdims=True) + EPS)
    return (y * gain) * (1.0 + scale) + shift


MXU_WIDTH = 256


def _ff_chunks(d_ff, target):
    assert d_ff % MXU_WIDTH == 0 and target % MXU_WIDTH == 0
    out, c0 = [], 0
    while c0 < d_ff:
        cn = min(target, d_ff - c0)
        out.append((c0, cn))
        c0 += cn
    return tuple(out)


def _ffn_kernel(*refs, base, mode, chunks):
    x_ref, mod_ref, g_ref, wi_ref, wo_ref, g2_ref = refs[:6]
    d_ff = wo_ref.shape[0]
    m = mod_ref[0]
    x = x_ref[...]
    if mode != "first":
        osb_ref, ogd_ref, ohg_ref, wsb_ref, wgd_ref, whg_ref, o_ref = refs[6:]
        y = (_dot(osb_ref[...], wsb_ref[...]) + _dot(ogd_ref[...], wgd_ref[...])
             + _dot(ohg_ref[...], whg_ref[...]))
        x = x + m[base - 1:base] * y
    else:
        o_ref, h_ref = refs[6:]
    h = _norm_mod(x, g_ref[...], m[base:base + 1], m[base + 1:base + 2]).astype(BF16)
    acc = None
    for c0, cn in chunks:
        gate = _dot(h, wi_ref[:, c0:c0 + cn])
        up = _dot(h, wi_ref[:, d_ff + c0:d_ff + c0 + cn])
        t = _dot((_silu(gate) * up).astype(BF16), wo_ref[c0:c0 + cn, :])
        acc = t if acc is None else acc + t
    out = x + (HALF_STEP * m[base + 2:base + 3]) * acc
    if mode == "last":
        out = (out * lax.rsqrt(jnp.mean(out * out, axis=-1, keepdims=True) + EPS)) * g2_ref[...]
    o_ref[...] = out
    if mode == "first":
        h_ref[...] = _norm_mod(out, g2_ref[...], m[base + 3:base + 4], m[base + 4:base + 5]).astype(BF16)


def _ffn(xf, mod_l, gain, w_in, w_out, gain2, mixer=None, *, seq, base, mode):
    bt, d = xf.shape
    d_ff = w_out.shape[0]
    tm = 512
    assert bt % tm == 0 and seq % tm == 0
    row = lambda n: pl.BlockSpec((tm, n), lambda i: (i, 0))
    resident = lambda a: pl.BlockSpec(a.shape, lambda i: (0, 0), pipeline_mode=pl.Buffered(1))
    in_specs = [
        row(d),
        pl.BlockSpec((1, N_MOD, d), lambda i: ((i * tm) // seq, 0, 0)),
        pl.BlockSpec((1, d), lambda i: (0, 0)),
        resident(w_in),
        resident(w_out),
        pl.BlockSpec((1, d), lambda i: (0, 0)),
    ]
    args = [xf, mod_l, gain.reshape(1, d), w_in, w_out, gain2.reshape(1, d)]
    out_specs = row(d)
    out_shape = jax.ShapeDtypeStruct((bt, d), F32)
    if mode == "first":
        out_specs = [row(d), row(d)]
        out_shape = [out_shape, jax.ShapeDtypeStruct((bt, d), BF16)]
    else:
        in_specs += [row(a.shape[1]) for a in mixer[:3]] + [resident(a) for a in mixer[3:]]
        args += list(mixer)
    return pl.pallas_call(
        functools.partial(_ffn_kernel, base=base, mode=mode, chunks=_ff_chunks(d_ff, 1024)),
        grid=(bt // tm,),
        in_specs=in_specs,
        out_specs=out_specs,
        out_shape=out_shape,
        compiler_params=_params(("arbitrary",)),
        name="ffn_half_step_" + mode,
    )(*args)


HALO = SUBLANES


def _inproj_kernel(h_ref, wsb_ref, wgd_ref, whg_ref, wba_ref, cw_ref, par_ref,
                   sb_ref, gd_ref, hg_ref, ba_ref, ext_sc, *, tiles_per_seq):
    tm = h_ref.shape[0]
    wq = GDN_HEADS * GDN_DK
    nqkv = 3 * wq

    @pl.when(pl.program_id(0) % tiles_per_seq == 0)
    def _():
        ext_sc[0:HALO, :] = jnp.zeros((HALO, nqkv), F32)

    h = h_ref[...]

    def conv_part(part, g):
        c0 = part * wq
        ext_sc[HALO:HALO + tm, c0:c0 + wq] = g
        for hd in range(GDN_HEADS):
            cols = slice(c0 + hd * GDN_DK, c0 + (hd + 1) * GDN_DK)
            y = cw_ref[GDN_CONV - 1:GDN_CONV, cols] * g[:, hd * GDN_DK:(hd + 1) * GDN_DK]
            for i in range(1, GDN_CONV):
                y = y + cw_ref[GDN_CONV - 1 - i:GDN_CONV - i, cols] * ext_sc[HALO - i:HALO - i + tm, cols]
            y = _silu(y)
            if part < 2:
                y = y * lax.rsqrt(jnp.sum(y * y, axis=-1, keepdims=True) + EPS)
                if part == 0:
                    y = y * (GDN_DK ** -0.5)
            gd_ref[:, cols] = y
        ext_sc[0:HALO, c0:c0 + wq] = ext_sc[tm:tm + HALO, c0:c0 + wq]

    nq = SB_HEADS * SB_DIM
    g_q = _dot(h, wgd_ref[:, 0:wq])
    g_k = _dot(h, wgd_ref[:, wq:2 * wq])
    conv_part(0, g_q)
    g_v = _dot(h, wgd_ref[:, 2 * wq:nqkv])
    conv_part(1, g_k)
    gd_ref[:, nqkv:] = _dot(h, wgd_ref[:, nqkv:])
    sb = _dot(h, wsb_ref[...])
    conv_part(2, g_v)
    sb = jnp.where(_iota(sb.shape, 1) < nq, sb * SB_Q_SCALE, sb)
    sb_ref[...] = sb.astype(BF16)
    hg_ref[...] = _dot(h, whg_ref[...])

    ba = _dot(h, wba_ref[...])
    log_a = -jnp.exp(par_ref[0:1, :]) * _softplus(ba + par_ref[1:2, :])
    ba_ref[...] = jnp.where(_iota(ba.shape, 1) < GDN_HEADS, jax.nn.sigmoid(ba), log_a)


def _inproj(hf, w_sb, w_gd, w_hg, w_ba, conv_w, a_log, dt_bias, *, seq):
    bt, d = hf.shape
    tm = 512
    assert bt % tm == 0 and seq % tm == 0
    n_sb, n_gd, n_hg, n_ba = w_sb.shape[1], w_gd.shape[1], w_hg.shape[1], w_ba.shape[1]
    par = jnp.zeros((SUBLANES, LANES), F32)
    par = par.at[0, GDN_HEADS:2 * GDN_HEADS].set(a_log).at[1, GDN_HEADS:2 * GDN_HEADS].set(dt_bias)
    row = lambda n: pl.BlockSpec((tm, n), lambda i: (i, 0))
    full = lambda a: pl.BlockSpec(a.shape, lambda i: (0, 0))
    return pl.pallas_call(
        functools.partial(_inproj_kernel, tiles_per_seq=seq // tm),
        grid=(bt // tm,),
        in_specs=[
            row(d),
            full(w_sb), full(w_gd), full(w_hg), full(w_ba), full(conv_w), full(par),
        ],
        out_specs=[row(n_sb), row(n_gd), row(n_hg), row(n_ba)],
        out_shape=[
            jax.ShapeDtypeStruct((bt, n_sb), BF16),
            jax.ShapeDtypeStruct((bt, n_gd), F32),
            jax.ShapeDtypeStruct((bt, n_hg), F32),
            jax.ShapeDtypeStruct((bt, n_ba), F32),
        ],
        scratch_shapes=[pltpu.VMEM((tm + HALO, 3 * GDN_HEADS * GDN_DK), F32)],
        compiler_params=_params(("arbitrary",)),
        name="mixer_in_proj",
    )(hf, w_sb, w_gd, w_hg, w_ba, conv_w, par)


SB_TQ = 512
SB_TC = 256


def _sb_kernel(q_ref, k_ref, v_ref, gain_ref, o_ref, *, seq):
    tq, tc = SB_TQ, SB_TC
    lane = _iota((1, LANES), 1)
    u = jnp.where(_iota((tc, tc), 0) >= _iota((tc, tc), 1), 1.0, 0.0).astype(BF16)
    hh = _iota((LANES, LANES), 0) // SB_DIM
    hc = _iota((LANES, LANES), 1) // SB_DIM
    bd = jnp.where(hh == hc, 1.0, 0.0).astype(BF16)
    causal = _iota((tc, tc), 1) < _iota((tc, tc), 0)

    def step(qs, k0s, nk, carries, accs, mask):
        ks = [k_ref[pl.ds(k0, nk), :] for k0 in k0s]
        vs = [v_ref[pl.ds(k0, nk), :] for k0 in k0s]
        zs = [_dot_nt(q, k) for q, k in zip(qs, ks)]
        sps = [jnp.maximum(z, 0.0) + jnp.log2(1.0 + jnp.exp2(-jnp.abs(z))) for z in zs]
        if mask is not None:
            sps = [jnp.where(mask, sp, 0.0) for sp in sps]
        ws = [[None] * (nk // tc) for _ in qs]
        for g in reversed(range(nk // tc)):
            cols = slice(g * tc, (g + 1) * tc)
            rs = [_dot(sp[:, cols].astype(BF16), u) + c for sp, c in zip(sps, carries)]
            for i, (z, r) in enumerate(zip(zs, rs)):
                ws[i][g] = jnp.exp2(z[:, cols] - r)
            carries = [c + jnp.sum(sp[:, cols], axis=-1, keepdims=True) for sp, c in zip(sps, carries)]
        ws = [w[0] if len(w) == 1 else jnp.concatenate(w, axis=1) for w in ws]
        if mask is not None:
            ws = [jnp.where(mask, w, 0.0) for w in ws]
        accs = [a + _dot(w.astype(BF16), v) for a, w, v in zip(accs, ws, vs)]
        return carries, accs

    nhp = LANES // SB_DIM
    nparts = tq // tc

    def qblock(qi, _):
        q0 = pl.multiple_of(qi * tq, tq)
        q2 = q_ref[pl.ds(q0, tq), :]
        hms = [(lane // SB_DIM) == h for h in range(nhp)]
        qhs = [jnp.where(hm, q2, jnp.zeros_like(q2)) for hm in hms]
        qps = [qh[p * tc:(p + 1) * tc] for qh in qhs for p in range(nparts)]
        k0s = [pl.multiple_of(q0 + p * tc, tc) for _ in qhs for p in range(nparts)]
        cs = [jnp.zeros((tc, 1), F32) for _ in qps]
        accs = [jnp.zeros((tc, LANES), F32) for _ in qps]
        cs, accs = step(qps, k0s, tc, cs, accs, causal)
        for back in range(1, nparts):
            sel = [i for i in range(len(qps)) if i % nparts >= back]
            c2, a2 = step([qps[i] for i in sel],
                          [pl.multiple_of(q0 + (i % nparts - back) * tc, tc) for i in sel], tc,
                          [cs[i] for i in sel], [accs[i] for i in sel], None)
            for j, i in enumerate(sel):
                cs[i], accs[i] = c2[j], a2[j]
        carries = [jnp.concatenate(cs[h * nparts:(h + 1) * nparts], axis=0) for h in range(nhp)]
        accs = [jnp.concatenate(accs[h * nparts:(h + 1) * nparts], axis=0) for h in range(nhp)]

        out = jnp.zeros((tq, LANES), F32)
        for h in range(nhp):
            def body(i, ca, qh=qhs[h]):
                k0 = pl.multiple_of((qi - 1 - i) * tq, tq)
                c2, a2 = step([qh], [k0], tq, [ca[0]], [ca[1]], None)
                return c2[0], a2[0]

            _, acc = lax.fori_loop(0, qi, body, (carries[h], accs[h]))
            out = out + jnp.where(hms[h], acc, 0.0)
        ms = _dot_exact_rhs(out * out, bd, 2) * (1.0 / SB_DIM)
        y = (out * lax.rsqrt(ms + EPS)) * gain_ref[...]
        o_ref[pl.ds(q0, tq), :] = y.astype(BF16)
        return 0

    lax.fori_loop(0, seq // tq, qblock, 0)


def _sb_attention(sb, gain, *, bsz, seq):
    npair = SB_HEADS * SB_DIM // LANES
    gain2 = jnp.tile(gain.reshape(1, SB_DIM), (1, LANES // SB_DIM))
    blk = lambda off: pl.BlockSpec((seq, LANES), lambda b, p: (b, off + p))
    return pl.pallas_call(
        functools.partial(_sb_kernel, seq=seq),
        grid=(bsz, npair),
        in_specs=[blk(0), blk(npair), blk(2 * npair), pl.BlockSpec((1, LANES), lambda b, p: (0, 0))],
        out_specs=pl.BlockSpec((seq, LANES), lambda b, p: (b, p)),
        out_shape=jax.ShapeDtypeStruct((bsz * seq, npair * LANES), BF16),
        compiler_params=_params(("arbitrary", "arbitrary")),
        name="sb_attention",
    )(sb, sb, sb, gain2)


GDN_TT = 256


def _gdn_kernel(gd_ref, ba_ref, gain_ref, o_ref, bl_sc, s_sc):
    tt = GDN_TT
    c = GDN_CHUNK
    nh = GDN_HEADS
    dk = GDN_DK
    wq = nh * dk
    nqkv = 3 * wq
    hpu = LANES // c
    hc = hpu * c
    ngr = nh // hpu
    assert hc == LANES and nh % hpu == 0
    qkv_sc = gd_ref

    @pl.when(pl.program_id(1) == 0)
    def _():
        s_sc[...] = jnp.zeros_like(s_sc)

    ba = ba_ref[...]
    lane = _iota((1, LANES), 1)
    for j in range(2 * nh):
        col = jnp.sum(jnp.where(lane == j, ba, 0.0), axis=-1, keepdims=True)
        bl_sc[j] = jnp.broadcast_to(col, (tt, LANES))

    r = _iota((hc, hc), 0)
    s = _iota((hc, hc), 1)
    same_head = (r // c) == (s // c)
    incl = same_head & (s <= r)
    strict = same_head & (s < r)
    tri = jnp.where(incl, 1.0, 0.0).astype(BF16)
    eye = jnp.where(r == s, 1.0, 0.0)

    rxs = r ^ s
    lmask = {m: jnp.where((s < r) & (rxs >= m) & (rxs < 2 * m), 1.0, 0.0).astype(BF16)
             for m in (1, 2, 4, 8, 16, 32)}
    assert 2 * max(lmask) == c

    nch = tt // c
    rows_of = [pl.ds(ci * c, c) for ci in range(nch)]
    units = [(ci, gi) for ci in range(nch) for gi in range(ngr)]
    heads_of = lambda gi: range(gi * hpu, (gi + 1) * hpu)

    def stack(unit, off):
        ci, gi = unit
        return jnp.concatenate(
            [qkv_sc[rows_of[ci], off + h * dk:off + (h + 1) * dk] for h in heads_of(gi)], axis=0)

    def stack_col(unit, j0):
        ci, gi = unit
        return jnp.concatenate([bl_sc[j0 + h, rows_of[ci], :] for h in heads_of(gi)], axis=0)

    q_st = [stack(un, 0) for un in units]
    k_st = [stack(un, wq) for un in units]
    v_st = [stack(un, 2 * wq) for un in units]
    beta_st = [stack_col(un, 0) for un in units]
    la_bc = [stack_col(un, nh) for un in units]
    g_bc = [_dot_exact_lhs(tri, la, 3) for la in la_bc]
    g_last = [jnp.concatenate(
        [jnp.broadcast_to(g[j * c + c - 1:j * c + c, :], (c, LANES)) for j in range(hpu)], axis=0)
        for g in g_bc]
    ex = [jnp.exp(g - jnp.broadcast_to(g.T[0:1, :], (hc, hc))) for g in g_bc]
    kb_st = [k * b for k, b in zip(k_st, beta_st)]
    k_bf = [k.astype(BF16) for k in k_st]
    a_kk = [(_dot_nt(kb.astype(BF16), kf) * jnp.where(strict, e, 0.0)).astype(BF16)
            for kb, kf, e in zip(kb_st, k_bf, ex)]
    a_qk = [(_dot_nt(q.astype(BF16), kf) * jnp.where(incl, e, 0.0)).astype(BF16)
            for q, kf, e in zip(q_st, k_bf, ex)]

    xs = [eye - (a * lmask[1]).astype(F32) for a in a_kk]
    m = 2
    while m < c:
        xb = [x.astype(BF16) for x in xs]
        ys = [_dot(a * lmask[m], b).astype(BF16) for a, b in zip(a_kk, xb)]
        xs = [x - _dot(b, y) for x, b, y in zip(xs, xb, ys)]
        m *= 2

    eg = [jnp.exp(g) for g in g_bc]
    uw = [_dot(x.astype(BF16), jnp.concatenate([v * b, kb * e], axis=1).astype(BF16))
          for x, v, b, kb, e in zip(xs, v_st, beta_st, kb_st, eg)]
    q_decs = [(q * e).astype(BF16) for q, e in zip(q_st, eg)]
    k_decs = [(k * jnp.exp(gl - g)).astype(BF16) for k, gl, g in zip(k_st, g_last, g_bc)]
    cds = [jnp.exp(gl) for gl in g_last]

    hs = [slice(j * c, (j + 1) * c) for j in range(hpu)]
    u_bf = [x[:, :dk].astype(BF16) for x in uw]
    w_bf = [x[:, dk:].astype(BF16) for x in uw]
    unit_of = lambda ci, h: (ci * ngr + h // hpu, hs[h % hpu])
    ktu = [[None] * nh for _ in range(nch)]
    ktw = [[None] * nh for _ in range(nch)]
    for ci in range(nch):
        for h in range(nh):
            un, sl = unit_of(ci, h)
            ktu[ci][h] = _dot_tn(k_decs[un][sl], u_bf[un][sl])
            ktw[ci][h] = _dot_tn(k_decs[un][sl], w_bf[un][sl]).astype(BF16)
    s_cur = [s_sc[h] for h in range(nh)]
    s_in = []
    for ci in range(nch):
        s_bf = [s.astype(BF16) for s in s_cur]
        s_in.append(s_bf)
        nxt = []
        for h, s in enumerate(s_cur):
            un, sl = unit_of(ci, h)
            nxt.append(s * cds[un][sl][0:1, :] + ktu[ci][h] - _dot(ktw[ci][h], s_bf[h]))
        s_cur = nxt
    for h in range(nh):
        s_sc[h] = s_cur[h]

    o_un = []
    for un, (ci, gi) in enumerate(units):
        v_new_st = jnp.concatenate(
            [uw[un][sl, :dk] - _dot(w_bf[un][sl], s_in[ci][h]) for h, sl in zip(heads_of(gi), hs)], axis=0)
        o_st = jnp.concatenate([_dot(q_decs[un][sl], s_in[ci][h]) for h, sl in zip(heads_of(gi), hs)], axis=0)
        o_st = o_st + _dot(a_qk[un], v_new_st.astype(BF16))
        o_un.append((o_st * lax.rsqrt(jnp.mean(o_st * o_st, axis=-1, keepdims=True) + EPS)) * gain_ref[...])
    for ci in range(nch):
        rows = rows_of[ci]
        o_t = jnp.concatenate(
            [o_un[ci * ngr + gi][sl] for gi in range(ngr) for sl in hs], axis=1)
        z = gd_ref[rows, nqkv:nqkv + wq]
        o_ref[rows, :] = (o_t * _silu(z)).astype(BF16)


def _gdn(gd, ba, gain, *, bsz, seq):
    tt = GDN_TT
    nt = seq // tt
    wq = GDN_HEADS * GDN_DK
    return pl.pallas_call(
        _gdn_kernel,
        grid=(bsz, nt),
        in_specs=[
            pl.BlockSpec((tt, gd.shape[1]), lambda b, t: (b * nt + t, 0)),
            pl.BlockSpec((tt, LANES), lambda b, t: (b * nt + t, 0)),
            pl.BlockSpec((1, GDN_DV), lambda b, t: (0, 0)),
        ],
        out_specs=pl.BlockSpec((tt, wq), lambda b, t: (b * nt + t, 0)),
        out_shape=jax.ShapeDtypeStruct((bsz * seq, wq), BF16),
        scratch_shapes=[
            pltpu.VMEM((2 * GDN_HEADS, tt, LANES), F32),
            pltpu.VMEM((GDN_HEADS, GDN_DK, GDN_DV), F32),
        ],
        compiler_params=_params(("arbitrary", "arbitrary")),
        name="gated_deltanet",
    )(gd, ba, gain.reshape(1, GDN_DV))


HG_TT = 256
NEG_BIG = -1e30


def _hg_kernel(hg_ref, lbl_ref, gain_ref, o_ref, st_sc, *, layer):
    tt = HG_TT
    c = HG_CHUNK
    w = HG_HEADS * HG_DK
    ti = pl.program_id(1)

    @pl.when(ti == 0)
    def _():
        st_sc[...] = jnp.zeros_like(st_sc)

    lg = lbl_ref[...]
    e = jnp.exp(lg - jnp.max(lg, axis=0, keepdims=True))
    p = e / jnp.sum(e, axis=0, keepdims=True)
    lb = jnp.zeros((1, w), F32)
    for i in range(1, layer + 1):
        lb = lb + p[i:i + 1, :]
    log_lb = jnp.log(lb)
    log1m_lb = jnp.log1p(-lb)

    hr = _iota((w, w), 0) // HG_DK
    hcn = _iota((w, w), 1) // HG_DK
    same_head = hr == hcn
    bd = jnp.where(same_head, 1.0, 0.0).astype(BF16)
    tr = _iota((c, c), 0)
    ts = _iota((c, c), 1)
    tri = jnp.where(ts <= tr, 1.0, 0.0).astype(BF16)
    levels = (SUBLANES, 2 * SUBLANES, 4 * SUBLANES)
    assert 2 * levels[-1] == c
    trow = _iota((c, w), 0)
    second_half = {m: ((trow // m) % 2) == 1 for m in levels}
    lane_head = _iota((1, w), 1) // HG_DK
    head_rows = (_iota((HG_HEADS * c, w), 0) // c) == (_iota((HG_HEADS * c, w), 1) // HG_DK)
    st_t = _iota((HG_HEADS * c, c), 0) % c
    st_s = _iota((HG_HEADS * c, c), 1)
    same_block = {m: (st_t // (2 * m)) == (st_s // (2 * m)) for m in levels}

    nch = tt // c
    rows_of = [pl.ds(ci * c, c) for ci in range(nch)]
    q = [_silu(hg_ref[rw, 0:w]) for rw in rows_of]
    f_pre = [hg_ref[rw, w:2 * w] for rw in rows_of]
    v = [hg_ref[rw, 2 * w:3 * w] for rw in rows_of]
    v_bf = [x.astype(BF16) for x in v]
    log_sig = [jnp.minimum(f, 0.0) - jnp.log1p(jnp.exp(-jnp.abs(f))) for f in f_pre]
    bterm = [log1m_lb + x for x in log_sig]
    lf = [jnp.maximum(log_lb, x) + jnp.log1p(jnp.exp(-jnp.abs(log_lb - x))) for x in bterm]
    k_in = [(1.0 - lb) * jax.nn.sigmoid(-f) for f in f_pre]
    b = [_dot_exact_lhs(tri, x, 3) for x in lf]
    b_last = [x[c - 1:c, :] for x in b]

    k_dec = [(k * jnp.exp(bl - x)).astype(BF16) for k, bl, x in zip(k_in, b_last, b)]
    upd = [jnp.where(same_head, _dot_tn(vb, kd), 0.0) for vb, kd in zip(v_bf, k_dec)]
    dec = [jnp.exp(bl) for bl in b_last]
    st = st_sc[...]
    st_in = []
    for ci in range(nch):
        st_in.append(st.astype(BF16))
        st = st * dec[ci] + upd[ci]
    st_sc[...] = st
    o = [_dot_nt((qq * jnp.exp(x)).astype(BF16), s) for qq, x, s in zip(q, b, st_in)]

    a_st = [None] * nch
    for m in levels:
        br = [jnp.concatenate(
            [jnp.broadcast_to(x[g0 + m:g0 + m + 1, :], (2 * m, w)) for g0 in range(0, c, 2 * m)], axis=0)
            for x in b]
        e = [jnp.exp(-jnp.abs(x - r)) for x, r in zip(b, br)]
        second = second_half[m]
        qe = [jnp.where(second, qq * ee, 0.0).astype(BF16) for qq, ee in zip(q, e)]
        ke = [jnp.where(second, 0.0, kk * ee).astype(BF16) for kk, ee in zip(k_in, e)]
        q_stk = [jnp.where(head_rows, jnp.concatenate([x] * HG_HEADS, axis=0), jnp.zeros((), BF16)) for x in qe]
        am = [_dot_nt(qs, kk) for qs, kk in zip(q_stk, ke)]
        if 2 * m < c:
            am = [jnp.where(same_block[m], x, 0.0) for x in am]
        a_st = [x if a is None else a + x for a, x in zip(a_st, am)]
    o_stk = [_dot(a.astype(BF16), vb) for a, vb in zip(a_st, v_bf)]
    for h in range(HG_HEADS):
        o = [oo + jnp.where(lane_head == h, os[h * c:(h + 1) * c], 0.0) for oo, os in zip(o, o_stk)]

    nt8 = c // SUBLANES
    sub = _iota((SUBLANES, 1), 0)
    ws = [[] for _ in range(nch)]
    for g in range(nt8):
        t0 = g * SUBLANES
        for i in range(SUBLANES):
            ok = sub >= i
            for ci in range(nch):
                qt, bt = q[ci][t0:t0 + SUBLANES, :], b[ci][t0:t0 + SUBLANES, :]
                diff = bt - b[ci][t0 + i:t0 + i + 1, :]
                ex = jnp.exp(diff if i == 0 else jnp.where(ok, diff, NEG_BIG))
                ws[ci].append((qt * ex) * k_in[ci][t0 + i:t0 + i + 1, :])
    cw = [_dot(jnp.concatenate(x, axis=0).astype(BF16), bd) for x in ws]
    for ci in range(nch):
        o_tiles = []
        for g in range(nt8):
            acc = None
            for i in range(SUBLANES):
                s = g * SUBLANES + i
                t = cw[ci][s * SUBLANES:(s + 1) * SUBLANES] * v[ci][s:s + 1, :]
                acc = t if acc is None else acc + t
            o_tiles.append(acc)
        o[ci] = o[ci] + jnp.concatenate(o_tiles, axis=0)
    ms = [_dot_exact_rhs(x * x, bd, 2) * (1.0 / HG_DV) for x in o]
    for ci in range(nch):
        y = (o[ci] * lax.rsqrt(ms[ci] + EPS)) * gain_ref[...]
        o_ref[rows_of[ci], :] = (y * _silu(hg_ref[rows_of[ci], 3 * w:4 * w])).astype(BF16)


def _hgrn2(hg, lb_logits, gain, *, bsz, seq, layer):
    tt = HG_TT
    nt = seq // tt
    w = HG_HEADS * HG_DV
    gain4 = jnp.tile(gain.reshape(1, HG_DV), (1, HG_HEADS))
    return pl.pallas_call(
        functools.partial(_hg_kernel, layer=layer),
        grid=(bsz, nt),
        in_specs=[
            pl.BlockSpec((tt, hg.shape[1]), lambda b, t: (b * nt + t, 0)),
            pl.BlockSpec(lb_logits.shape, lambda b, t: (0, 0)),
            pl.BlockSpec((1, w), lambda b, t: (0, 0)),
        ],
        out_specs=pl.BlockSpec((tt, w), lambda b, t: (b * nt + t, 0)),
        out_shape=jax.ShapeDtypeStruct((bsz * seq, w), BF16),
        scratch_shapes=[pltpu.VMEM((w, w), F32)],
        compiler_params=_params(("arbitrary", "arbitrary")),
        name="hgrn2",
    )(hg, lb_logits, gain4)


def kernel(x, c, ffn1_norm, ffn1_w_in, ffn1_w_out, mix_norm, mix_w_in, gdn_conv_w, gdn_A_log, gdn_dt_bias,
           sb_out_norm, gdn_out_norm, hg_out_norm, hg_lb_logits, mix_w_out, ffn2_norm, ffn2_w_in, ffn2_w_out,
           ada_w, ada_b, final_norm):
    bsz, seq, d = x.shape
    depth = ada_w.shape[0]
    sb_w = SB_HEADS * SB_DIM
    gd_wk = GDN_HEADS * GDN_DK
    gd_wv = GDN_HEADS * GDN_DV
    hg_w = HG_HEADS * HG_DK

    mod = _ada(c, ada_w, ada_b)
    xf = x.reshape(bsz * seq, d)
    for l in range(depth):
        mod_l = mod[l].reshape(bsz, N_MOD, d)
        xf, hf = _ffn(xf, mod_l, ffn1_norm[l], ffn1_w_in[l].astype(BF16), ffn1_w_out[l].astype(BF16),
                      mix_norm[l], seq=seq, base=0, mode="first")

        wi = mix_w_in[l].astype(BF16)
        o_gd = 3 * sb_w
        o_b = o_gd + 3 * gd_wk + gd_wv
        o_hg = o_b + 2 * GDN_HEADS
        w_ba = jnp.pad(wi[:, o_b:o_hg], ((0, 0), (0, LANES - 2 * GDN_HEADS)))
        sb, gd, hg, ba = _inproj(hf, wi[:, :o_gd], wi[:, o_gd:o_b], wi[:, o_hg:], w_ba,
                                 gdn_conv_w[l], gdn_A_log[l], gdn_dt_bias[l], seq=seq)

        o_sb = _sb_attention(sb, sb_out_norm[l], bsz=bsz, seq=seq)
        o_gdn = _gdn(gd, ba, gdn_out_norm[l], bsz=bsz, seq=seq)
        o_hgr = _hgrn2(hg, hg_lb_logits, hg_out_norm[l], bsz=bsz, seq=seq, layer=l)

        wo = mix_w_out[l].astype(BF16)
        mixer = (o_sb, o_gdn, o_hgr, wo[:sb_w], wo[sb_w:sb_w + gd_wv], wo[sb_w + gd_wv:])
        xf = _ffn(xf, mod_l, ffn2_norm[l], ffn2_w_in[l].astype(BF16), ffn2_w_out[l].astype(BF16),
                  final_norm, mixer, seq=seq, base=6, mode="last" if l == depth - 1 else "second")
    return xf.reshape(bsz, seq, d)
```

```python
import functools

import jax
import jax.numpy as jnp
from jax import lax
from jax.experimental import pallas as pl
from jax.experimental.pallas import tpu as pltpu

F32 = jnp.float32
BF16 = jnp.bfloat16

EPS = 1e-6
HALF_STEP = 0.5
N_MOD = 9

SB_HEADS, SB_DIM = 4, 64
LOG2E = 1.4426950408889634
SB_Q_SCALE = SB_DIM ** -0.5 * LOG2E
GDN_HEADS, GDN_DK, GDN_DV, GDN_CONV, GDN_CHUNK = 4, 128, 128, 4, 64
HG_HEADS, HG_DK, HG_DV, HG_CHUNK = 4, 64, 64, 64

LANES = 128
SUBLANES = 8
VMEM_LIMIT = 56 * 1024 * 1024

_NT = (((1,), (1,)), ((), ()))
_TN = (((0,), (0,)), ((), ()))


def _dot(a, b):
    return jnp.dot(a, b, preferred_element_type=F32)


def _dot_nt(a, b):
    return lax.dot_general(a, b, _NT, preferred_element_type=F32)


def _dot_tn(a, b):
    return lax.dot_general(a, b, _TN, preferred_element_type=F32)


def _split(x, parts):
    out = []
    r = x
    for _ in range(parts):
        p = r.astype(BF16)
        out.append(p)
        r = r - p.astype(F32)
    return out


def _dot_exact_rhs(x, m, parts):
    acc = None
    for p in _split(x, parts):
        t = _dot(p, m)
        acc = t if acc is None else acc + t
    return acc


def _dot_exact_lhs(m, x, parts):
    acc = None
    for p in _split(x, parts):
        t = _dot(m, p)
        acc = t if acc is None else acc + t
    return acc


def _silu(x):
    return x * jax.nn.sigmoid(x)


def _softplus(x):
    return jnp.maximum(x, 0.0) + jnp.log1p(jnp.exp(-jnp.abs(x)))


def _iota(shape, dim):
    return lax.broadcasted_iota(jnp.int32, shape, dim)


def _params(sem):
    return pltpu.CompilerParams(dimension_semantics=sem, vmem_limit_bytes=VMEM_LIMIT)


def _ada_kernel(c_ref, w_ref, b_ref, o_ref):
    ca = _silu(c_ref[...]).astype(BF16)
    o_ref[0] = _dot(ca, w_ref[0].astype(BF16)) + b_ref[0]


def _ada(c, ada_w, ada_b):
    depth, d, n = ada_w.shape
    bsz = c.shape[0]
    tn = 1152
    assert n % tn == 0
    return pl.pallas_call(
        _ada_kernel,
        grid=(depth, n // tn),
        in_specs=[
            pl.BlockSpec((bsz, d), lambda l, j: (0, 0)),
            pl.BlockSpec((1, d, tn), lambda l, j: (l, 0, j)),
            pl.BlockSpec((1, 1, tn), lambda l, j: (l, 0, j)),
        ],
        out_specs=pl.BlockSpec((1, bsz, tn), lambda l, j: (l, 0, j)),
        out_shape=jax.ShapeDtypeStruct((depth, bsz, n), F32),
        compiler_params=_params(("arbitrary", "arbitrary")),
        name="ada_mod",
    )(c, ada_w, ada_b.reshape(depth, 1, n))


def _norm_mod(x, gain, shift, scale):
    y = x * lax.rsqrt(jnp.mean(x * x, axis=-1, keepdims=True) + EPS)
    return (y * gain) * (1.0 + scale) + shift


MXU_WIDTH = 256


def _ff_chunks(d_ff, target):
    assert d_ff % MXU_WIDTH == 0 and target % MXU_WIDTH == 0
    out, c0 = [], 0
    while c0 < d_ff:
        cn = min(target, d_ff - c0)
        out.append((c0, cn))
        c0 += cn
    return tuple(out)


def _ffn_kernel(*refs, base, mode, chunks):
    x_ref, mod_ref, g_ref, wi_ref, wo_ref, g2_ref = refs[:6]
    wi_sc, wo_sc = refs[-2:]
    refs = refs[:-2]
    d_ff = wo_sc.shape[0]
    step = pl.program_id(0)
    cw, cr = wi_ref.shape[1], wo_ref.shape[0]

    for j in range(W_CAST_STEPS):
        @pl.when(step == j)
        def _(j=j):
            wi_sc[:, j * cw:(j + 1) * cw] = wi_ref[...].astype(BF16)
            wo_sc[j * cr:(j + 1) * cr, :] = wo_ref[...].astype(BF16)

    @pl.when(step >= W_CAST_STEPS)
    def _():
        m = mod_ref[0]
        x = x_ref[...]
        if mode != "first":
            osb_ref, ogd_ref, ohg_ref, wsb_ref, wgd_ref, whg_ref, o_ref = refs[6:]
            y = (_dot(osb_ref[...], wsb_ref[...]) + _dot(ogd_ref[...], wgd_ref[...])
                 + _dot(ohg_ref[...], whg_ref[...]))
            x = x + m[base - 1:base] * y
        else:
            o_ref, h_ref = refs[6:]
        h = _norm_mod(x, g_ref[...], m[base:base + 1], m[base + 1:base + 2]).astype(BF16)
        acc = None
        for c0, cn in chunks:
            gate = _dot(h, wi_sc[:, c0:c0 + cn])
            up = _dot(h, wi_sc[:, d_ff + c0:d_ff + c0 + cn])
            t = _dot((_silu(gate) * up).astype(BF16), wo_sc[c0:c0 + cn, :])
            acc = t if acc is None else acc + t
        out = x + (HALF_STEP * m[base + 2:base + 3]) * acc
        if mode == "last":
            out = (out * lax.rsqrt(jnp.mean(out * out, axis=-1, keepdims=True) + EPS)) * g2_ref[...]
        o_ref[...] = out
        if mode == "first":
            h_ref[...] = _norm_mod(out, g2_ref[...], m[base + 3:base + 4], m[base + 4:base + 5]).astype(BF16)


W_CAST_STEPS = 11


def _ffn(xf, mod_l, gain, w_in, w_out, gain2, mixer=None, *, seq, base, mode):
    bt, d = xf.shape
    d_ff = w_out.shape[0]
    tm = 512
    nc = W_CAST_STEPS
    assert bt % tm == 0 and seq % tm == 0
    assert (2 * d_ff) % (nc * LANES) == 0 and d_ff % (nc * 2 * SUBLANES) == 0
    cw, cr = 2 * d_ff // nc, d_ff // nc
    tile = lambda i: jnp.maximum(i - nc, 0)
    slab = lambda i: jnp.minimum(i, nc - 1)
    row = lambda n: pl.BlockSpec((tm, n), lambda i: (tile(i), 0))
    resident = lambda a: pl.BlockSpec(a.shape, lambda i: (0, 0), pipeline_mode=pl.Buffered(1))
    in_specs = [
        row(d),
        pl.BlockSpec((1, N_MOD, d), lambda i: ((tile(i) * tm) // seq, 0, 0)),
        pl.BlockSpec((1, d), lambda i: (0, 0)),
        pl.BlockSpec((d, cw), lambda i: (0, slab(i))),
        pl.BlockSpec((cr, d), lambda i: (slab(i), 0)),
        pl.BlockSpec((1, d), lambda i: (0, 0)),
    ]
    args = [xf, mod_l, gain.reshape(1, d), w_in, w_out, gain2.reshape(1, d)]
    out_specs = row(d)
    out_shape = jax.ShapeDtypeStruct((bt, d), F32)
    if mode == "first":
        out_specs = [row(d), row(d)]
        out_shape = [out_shape, jax.ShapeDtypeStruct((bt, d), BF16)]
    else:
        in_specs += [row(a.shape[1]) for a in mixer[:3]] + [resident(a) for a in mixer[3:]]
        args += list(mixer)
    return pl.pallas_call(
        functools.partial(_ffn_kernel, base=base, mode=mode, chunks=_ff_chunks(d_ff, 1024)),
        grid=(nc + bt // tm,),
        in_specs=in_specs,
        out_specs=out_specs,
        out_shape=out_shape,
        scratch_shapes=[pltpu.VMEM((d, 2 * d_ff), BF16), pltpu.VMEM((d_ff, d), BF16)],
        compiler_params=_params(("arbitrary",)),
        name="ffn_half_step_" + mode,
    )(*args)


HALO = SUBLANES


def _inproj_kernel(h_ref, wsb_ref, wgd_ref, whg_ref, wba_ref, cw_ref, par_ref,
                   sb_ref, gd_ref, hg_ref, ba_ref, ext_sc, *, tiles_per_seq):
    tm = h_ref.shape[0]
    wq = GDN_HEADS * GDN_DK
    nqkv = 3 * wq

    @pl.when(pl.program_id(0) % tiles_per_seq == 0)
    def _():
        ext_sc[0:HALO, :] = jnp.zeros((HALO, nqkv), F32)

    h = h_ref[...]

    def conv_part(part, g):
        c0 = part * wq
        ext_sc[HALO:HALO + tm, c0:c0 + wq] = g
        for hd in range(GDN_HEADS):
            cols = slice(c0 + hd * GDN_DK, c0 + (hd + 1) * GDN_DK)
            y = cw_ref[GDN_CONV - 1:GDN_CONV, cols] * g[:, hd * GDN_DK:(hd + 1) * GDN_DK]
            for i in range(1, GDN_CONV):
                y = y + cw_ref[GDN_CONV - 1 - i:GDN_CONV - i, cols] * ext_sc[HALO - i:HALO - i + tm, cols]
            y = _silu(y)
            if part < 2:
                y = y * lax.rsqrt(jnp.sum(y * y, axis=-1, keepdims=True) + EPS)
                if part == 0:
                    y = y * (GDN_DK ** -0.5)
            gd_ref[:, cols] = y
        ext_sc[0:HALO, c0:c0 + wq] = ext_sc[tm:tm + HALO, c0:c0 + wq]

    nq = SB_HEADS * SB_DIM
    g_q = _dot(h, wgd_ref[:, 0:wq])
    g_k = _dot(h, wgd_ref[:, wq:2 * wq])
    conv_part(0, g_q)
    g_v = _dot(h, wgd_ref[:, 2 * wq:nqkv])
    conv_part(1, g_k)
    gd_ref[:, nqkv:] = _dot(h, wgd_ref[:, nqkv:])
    sb = _dot(h, wsb_ref[...])
    conv_part(2, g_v)
    sb = jnp.where(_iota(sb.shape, 1) < nq, sb * SB_Q_SCALE, sb)
    sb_ref[...] = sb.astype(BF16)
    hg_ref[...] = _dot(h, whg_ref[...])

    ba = _dot(h, wba_ref[...])
    log_a = -jnp.exp(par_ref[0:1, :]) * _softplus(ba + par_ref[1:2, :])
    ba_ref[...] = jnp.where(_iota(ba.shape, 1) < GDN_HEADS, jax.nn.sigmoid(ba), log_a)


def _inproj(hf, w_sb, w_gd, w_hg, w_ba, conv_w, a_log, dt_bias, *, seq):
    bt, d = hf.shape
    tm = 512
    assert bt % tm == 0 and seq % tm == 0
    n_sb, n_gd, n_hg, n_ba = w_sb.shape[1], w_gd.shape[1], w_hg.shape[1], w_ba.shape[1]
    par = jnp.zeros((SUBLANES, LANES), F32)
    par = par.at[0, GDN_HEADS:2 * GDN_HEADS].set(a_log).at[1, GDN_HEADS:2 * GDN_HEADS].set(dt_bias)
    row = lambda n: pl.BlockSpec((tm, n), lambda i: (i, 0))
    full = lambda a: pl.BlockSpec(a.shape, lambda i: (0, 0))
    return pl.pallas_call(
        functools.partial(_inproj_kernel, tiles_per_seq=seq // tm),
        grid=(bt // tm,),
        in_specs=[
            row(d),
            full(w_sb), full(w_gd), full(w_hg), full(w_ba), full(conv_w), full(par),
        ],
        out_specs=[row(n_sb), row(n_gd), row(n_hg), row(n_ba)],
        out_shape=[
            jax.ShapeDtypeStruct((bt, n_sb), BF16),
            jax.ShapeDtypeStruct((bt, n_gd), F32),
            jax.ShapeDtypeStruct((bt, n_hg), F32),
            jax.ShapeDtypeStruct((bt, n_ba), F32),
        ],
        scratch_shapes=[pltpu.VMEM((tm + HALO, 3 * GDN_HEADS * GDN_DK), F32)],
        compiler_params=_params(("arbitrary",)),
        name="mixer_in_proj",
    )(hf, w_sb, w_gd, w_hg, w_ba, conv_w, par)


SB_TQ = 512
SB_TC = 256


def _sb_kernel(q_ref, k_ref, v_ref, gain_ref, o_ref, *, seq):
    tq, tc = SB_TQ, SB_TC
    lane = _iota((1, LANES), 1)
    u = jnp.where(_iota((tc, tc), 0) >= _iota((tc, tc), 1), 1.0, 0.0).astype(BF16)
    hh = _iota((LANES, LANES), 0) // SB_DIM
    hc = _iota((LANES, LANES), 1) // SB_DIM
    bd = jnp.where(hh == hc, 1.0, 0.0).astype(BF16)
    causal = _iota((tc, tc), 1) < _iota((tc, tc), 0)

    def logits(qs, k0s, nk, mask):
        ks = [k_ref[pl.ds(k0, nk), :] for k0 in k0s]
        zs = [_dot_nt(q, k) for q, k in zip(qs, ks)]
        sps = [jnp.maximum(z, 0.0) + jnp.log2(1.0 + jnp.exp2(-jnp.abs(z))) for z in zs]
        if mask is not None:
            sps = [jnp.where(mask, sp, 0.0) for sp in sps]
        return zs, sps

    def accumulate(zs, sps, k0s, nk, carries, accs, mask):
        vs = [v_ref[pl.ds(k0, nk), :] for k0 in k0s]
        ws = [[None] * (nk // tc) for _ in zs]
        for g in reversed(range(nk // tc)):
            cols = slice(g * tc, (g + 1) * tc)
            rs = [_dot(sp[:, cols].astype(BF16), u) + c for sp, c in zip(sps, carries)]
            for i, (z, r) in enumerate(zip(zs, rs)):
                ws[i][g] = jnp.exp2(z[:, cols] - r)
            carries = [c + jnp.sum(sp[:, cols], axis=-1, keepdims=True) for sp, c in zip(sps, carries)]
        ws = [w[0] if len(w) == 1 else jnp.concatenate(w, axis=1) for w in ws]
        if mask is not None:
            ws = [jnp.where(mask, w, 0.0) for w in ws]
        accs = [a + _dot(w.astype(BF16), v) for a, w, v in zip(accs, ws, vs)]
        return carries, accs

    def step(qs, k0s, nk, carries, accs, mask):
        zs, sps = logits(qs, k0s, nk, mask)
        return accumulate(zs, sps, k0s, nk, carries, accs, mask)

    nhp = LANES // SB_DIM
    nparts = tq // tc

    def qblock(qi, _):
        q0 = pl.multiple_of(qi * tq, tq)
        q2 = q_ref[pl.ds(q0, tq), :]
        hms = [(lane // SB_DIM) == h for h in range(nhp)]
        qhs = [jnp.where(hm, q2, jnp.zeros_like(q2)) for hm in hms]
        qps = [qh[p * tc:(p + 1) * tc] for qh in qhs for p in range(nparts)]
        k0s = [pl.multiple_of(q0 + p * tc, tc) for _ in qhs for p in range(nparts)]
        cs = [jnp.zeros((tc, 1), F32) for _ in qps]
        accs = [jnp.zeros((tc, LANES), F32) for _ in qps]
        cs, accs = step(qps, k0s, tc, cs, accs, causal)
        for back in range(1, nparts):
            sel = [i for i in range(len(qps)) if i % nparts >= back]
            c2, a2 = step([qps[i] for i in sel],
                          [pl.multiple_of(q0 + (i % nparts - back) * tc, tc) for i in sel], tc,
                          [cs[i] for i in sel], [accs[i] for i in sel], None)
            for j, i in enumerate(sel):
                cs[i], accs[i] = c2[j], a2[j]
        carries = [jnp.concatenate(cs[h * nparts:(h + 1) * nparts], axis=0) for h in range(nhp)]
        accs = [jnp.concatenate(accs[h * nparts:(h + 1) * nparts], axis=0) for h in range(nhp)]

        out = jnp.zeros((tq, LANES), F32)
        for h in range(nhp):
            def body(i, ca, qh=qhs[h]):
                k0 = pl.multiple_of((qi - 1 - i) * tq, tq)
                c2, a2 = step([qh], [k0], tq, [ca[0]], [ca[1]], None)
                return c2[0], a2[0]

            _, acc = lax.fori_loop(0, qi, body, (carries[h], accs[h]))
            out = out + jnp.where(hms[h], acc, 0.0)
        ms = _dot_exact_rhs(out * out, bd, 2) * (1.0 / SB_DIM)
        y = (out * lax.rsqrt(ms + EPS)) * gain_ref[...]
        o_ref[pl.ds(q0, tq), :] = y.astype(BF16)
        return 0

    lax.fori_loop(0, seq // tq, qblock, 0)


def _sb_attention(sb, gain, *, bsz, seq):
    npair = SB_HEADS * SB_DIM // LANES
    gain2 = jnp.tile(gain.reshape(1, SB_DIM), (1, LANES // SB_DIM))
    blk = lambda off: pl.BlockSpec((seq, LANES), lambda b, p: (b, off + p))
    return pl.pallas_call(
        functools.partial(_sb_kernel, seq=seq),
        grid=(bsz, npair),
        in_specs=[blk(0), blk(npair), blk(2 * npair), pl.BlockSpec((1, LANES), lambda b, p: (0, 0))],
        out_specs=pl.BlockSpec((seq, LANES), lambda b, p: (b, p)),
        out_shape=jax.ShapeDtypeStruct((bsz * seq, npair * LANES), BF16),
        compiler_params=_params(("arbitrary", "arbitrary")),
        name="sb_attention",
    )(sb, sb, sb, gain2)


GDN_TT = 512


def _gdn_kernel(gd_ref, ba_ref, gain_ref, o_ref, bl_sc, s_sc):
    tt = GDN_TT
    c = GDN_CHUNK
    nh = GDN_HEADS
    dk = GDN_DK
    wq = nh * dk
    nqkv = 3 * wq
    hpu = LANES // c
    hc = hpu * c
    ngr = nh // hpu
    assert hc == LANES and nh % hpu == 0
    qkv_sc = gd_ref

    @pl.when(pl.program_id(1) == 0)
    def _():
        s_sc[...] = jnp.zeros_like(s_sc)

    ba = ba_ref[...]
    lane = _iota((1, LANES), 1)
    for j in range(2 * nh):
        col = jnp.sum(jnp.where(lane == j, ba, 0.0), axis=-1, keepdims=True)
        bl_sc[j] = jnp.broadcast_to(col, (tt, LANES))

    r = _iota((hc, hc), 0)
    s = _iota((hc, hc), 1)
    same_head = (r // c) == (s // c)
    incl = same_head & (s <= r)
    strict = same_head & (s < r)
    tri = jnp.where(incl, 1.0, 0.0).astype(BF16)
    eye = jnp.where(r == s, 1.0, 0.0)

    rxs = r ^ s
    lmask = {m: jnp.where((s < r) & (rxs >= m) & (rxs < 2 * m), 1.0, 0.0).astype(BF16)
             for m in (1, 2, 4, 8, 16, 32)}
    assert 2 * max(lmask) == c

    nch = tt // c
    rows_of = [pl.ds(ci * c, c) for ci in range(nch)]
    units = [(ci, gi) for ci in range(nch) for gi in range(ngr)]
    heads_of = lambda gi: range(gi * hpu, (gi + 1) * hpu)

    def stack(unit, off):
        ci, gi = unit
        return jnp.concatenate(
            [qkv_sc[rows_of[ci], off + h * dk:off + (h + 1) * dk] for h in heads_of(gi)], axis=0)

    def stack_col(unit, j0):
        ci, gi = unit
        return jnp.concatenate([bl_sc[j0 + h, rows_of[ci], :] for h in heads_of(gi)], axis=0)

    q_st = [stack(un, 0) for un in units]
    k_st = [stack(un, wq) for un in units]
    v_st = [stack(un, 2 * wq) for un in units]
    beta_st = [stack_col(un, 0) for un in units]
    la_bc = [stack_col(un, nh) for un in units]
    g_bc = [_dot_exact_lhs(tri, la, 3) for la in la_bc]
    g_last = [jnp.concatenate(
        [jnp.broadcast_to(g[j * c + c - 1:j * c + c, :], (c, LANES)) for j in range(hpu)], axis=0)
        for g in g_bc]
    ex = [jnp.exp(g - jnp.broadcast_to(g.T[0:1, :], (hc, hc))) for g in g_bc]
    kb_st = [k * b for k, b in zip(k_st, beta_st)]
    k_bf = [k.astype(BF16) for k in k_st]
    a_kk = [(_dot_nt(kb.astype(BF16), kf) * jnp.where(strict, e, 0.0)).astype(BF16)
            for kb, kf, e in zip(kb_st, k_bf, ex)]
    a_qk = [(_dot_nt(q.astype(BF16), kf) * jnp.where(incl, e, 0.0)).astype(BF16)
            for q, kf, e in zip(q_st, k_bf, ex)]

    xs = [eye - (a * lmask[1]).astype(F32) for a in a_kk]
    m = 2
    while m < c:
        xb = [x.astype(BF16) for x in xs]
        ys = [_dot(a * lmask[m], b).astype(BF16) for a, b in zip(a_kk, xb)]
        xs = [x - _dot(b, y) for x, b, y in zip(xs, xb, ys)]
        m *= 2

    eg = [jnp.exp(g) for g in g_bc]
    uw = [_dot(x.astype(BF16), jnp.concatenate([v * b, kb * e], axis=1).astype(BF16))
          for x, v, b, kb, e in zip(xs, v_st, beta_st, kb_st, eg)]
    q_decs = [(q * e).astype(BF16) for q, e in zip(q_st, eg)]
    k_decs = [(k * jnp.exp(gl - g)).astype(BF16) for k, gl, g in zip(k_st, g_last, g_bc)]
    cds = [jnp.exp(gl) for gl in g_last]

    hs = [slice(j * c, (j + 1) * c) for j in range(hpu)]
    u_bf = [x[:, :dk].astype(BF16) for x in uw]
    w_bf = [x[:, dk:].astype(BF16) for x in uw]
    unit_of = lambda ci, h: (ci * ngr + h // hpu, hs[h % hpu])
    assert nch % 2 == 0
    amap = [[None] * nh for _ in range(nch)]
    for ci in range(nch):
        for h in range(nh):
            un, sl = unit_of(ci, h)
            n_f = _dot_tn(k_decs[un][sl], w_bf[un][sl])
            amap[ci][h] = (cds[un][sl][0:1, :], n_f, n_f.astype(BF16), _dot_tn(k_decs[un][sl], u_bf[un][sl]))
    pair = [[None] * nh for _ in range(nch // 2)]
    for p in range(nch // 2):
        for h in range(nh):
            (a0, n0, n0b, q0), (a1, n1, n1b, q1) = amap[2 * p][h], amap[2 * p + 1][h]
            n01 = a1 * n0 + a0 * n1 - _dot(n1b, n0b)
            pair[p][h] = (a0 * a1, n01.astype(BF16), a1 * q0 + q1 - _dot(n1b, q0.astype(BF16)))
    s_cur = [s_sc[h] for h in range(nh)]
    s_in = [None] * nch
    for p in range(nch // 2):
        s_bf = [s.astype(BF16) for s in s_cur]
        s_in[2 * p] = s_bf
        s_in[2 * p + 1] = [(s * amap[2 * p][h][0] + amap[2 * p][h][3] - _dot(amap[2 * p][h][2], s_bf[h])
                            ).astype(BF16) for h, s in enumerate(s_cur)]
        s_cur = [s * pair[p][h][0] + pair[p][h][2] - _dot(pair[p][h][1], s_bf[h])
                 for h, s in enumerate(s_cur)]
    for h in range(nh):
        s_sc[h] = s_cur[h]

    v_new = [jnp.concatenate(
        [uw[un][sl, :dk] - _dot(w_bf[un][sl], s_in[ci][h]) for h, sl in zip(heads_of(gi), hs)], axis=0)
        for un, (ci, gi) in enumerate(units)]
    o_un = [jnp.concatenate([_dot(q_decs[un][sl], s_in[ci][h]) for h, sl in zip(heads_of(gi), hs)], axis=0)
            for un, (ci, gi) in enumerate(units)]
    o_un = [o + _dot(a, vn.astype(BF16)) for o, a, vn in zip(o_un, a_qk, v_new)]
    ms = [jnp.mean(o * o, axis=-1, keepdims=True) for o in o_un]
    o_un = [(o * lax.rsqrt(m_ + EPS)) * gain_ref[...] for o, m_ in zip(o_un, ms)]
    for ci in range(nch):
        rows = rows_of[ci]
        o_t = jnp.concatenate(
            [o_un[ci * ngr + gi][sl] for gi in range(ngr) for sl in hs], axis=1)
        z = gd_ref[rows, nqkv:nqkv + wq]
        o_ref[rows, :] = (o_t * _silu(z)).astype(BF16)


def _gdn(gd, ba, gain, *, bsz, seq):
    tt = GDN_TT
    nt = seq // tt
    wq = GDN_HEADS * GDN_DK
    return pl.pallas_call(
        _gdn_kernel,
        grid=(bsz, nt),
        in_specs=[
            pl.BlockSpec((tt, gd.shape[1]), lambda b, t: (b * nt + t, 0)),
            pl.BlockSpec((tt, LANES), lambda b, t: (b * nt + t, 0)),
            pl.BlockSpec((1, GDN_DV), lambda b, t: (0, 0)),
        ],
        out_specs=pl.BlockSpec((tt, wq), lambda b, t: (b * nt + t, 0)),
        out_shape=jax.ShapeDtypeStruct((bsz * seq, wq), BF16),
        scratch_shapes=[
            pltpu.VMEM((2 * GDN_HEADS, tt, LANES), F32),
            pltpu.VMEM((GDN_HEADS, GDN_DK, GDN_DV), F32),
        ],
        compiler_params=_params(("arbitrary", "arbitrary")),
        name="gated_deltanet",
    )(gd, ba, gain.reshape(1, GDN_DV))


HG_TT = 256
NEG_BIG = -1e30


def _hg_kernel(hg_ref, lbl_ref, gain_ref, o_ref, st_sc, *, layer):
    tt = HG_TT
    c = HG_CHUNK
    w = HG_HEADS * HG_DK
    ti = pl.program_id(1)

    @pl.when(ti == 0)
    def _():
        st_sc[...] = jnp.zeros_like(st_sc)

    lg = lbl_ref[...]
    e = jnp.exp(lg - jnp.max(lg, axis=0, keepdims=True))
    p = e / jnp.sum(e, axis=0, keepdims=True)
    lb = jnp.zeros((1, w), F32)
    for i in range(1, layer + 1):
        lb = lb + p[i:i + 1, :]
    log_lb = jnp.log(lb)
    log1m_lb = jnp.log1p(-lb)

    hr = _iota((w, w), 0) // HG_DK
    hcn = _iota((w, w), 1) // HG_DK
    same_head = hr == hcn
    bd = jnp.where(same_head, 1.0, 0.0).astype(BF16)
    tr = _iota((c, c), 0)
    ts = _iota((c, c), 1)
    tri = jnp.where(ts <= tr, 1.0, 0.0).astype(BF16)
    levels = (SUBLANES, 2 * SUBLANES, 4 * SUBLANES)
    assert 2 * levels[-1] == c
    trow = _iota((c, w), 0)
    second_half = {m: ((trow // m) % 2) == 1 for m in levels}
    lane_head = _iota((1, w), 1) // HG_DK
    head_rows = (_iota((HG_HEADS * c, w), 0) // c) == (_iota((HG_HEADS * c, w), 1) // HG_DK)
    st_t = _iota((HG_HEADS * c, c), 0) % c
    st_s = _iota((HG_HEADS * c, c), 1)
    same_block = {m: (st_t // (2 * m)) == (st_s // (2 * m)) for m in levels}

    nch = tt // c
    rows_of = [pl.ds(ci * c, c) for ci in range(nch)]
    q = [_silu(hg_ref[rw, 0:w]) for rw in rows_of]
    f_pre = [hg_ref[rw, w:2 * w] for rw in rows_of]
    v = [hg_ref[rw, 2 * w:3 * w] for rw in rows_of]
    v_bf = [x.astype(BF16) for x in v]
    log_sig = [jnp.minimum(f, 0.0) - jnp.log1p(jnp.exp(-jnp.abs(f))) for f in f_pre]
    bterm = [log1m_lb + x for x in log_sig]
    lf = [jnp.maximum(log_lb, x) + jnp.log1p(jnp.exp(-jnp.abs(log_lb - x))) for x in bterm]
    k_in = [(1.0 - lb) * jax.nn.sigmoid(-f) for f in f_pre]
    b = [_dot_exact_lhs(tri, x, 3) for x in lf]
    b_last = [x[c - 1:c, :] for x in b]

    k_dec = [(k * jnp.exp(bl - x)).astype(BF16) for k, bl, x in zip(k_in, b_last, b)]
    upd = [jnp.where(same_head, _dot_tn(vb, kd), 0.0) for vb, kd in zip(v_bf, k_dec)]
    dec = [jnp.exp(bl) for bl in b_last]
    st = st_sc[...]
    st_in = []
    for ci in range(nch):
        st_in.append(st.astype(BF16))
        st = st * dec[ci] + upd[ci]
    st_sc[...] = st
    o = [_dot_nt((qq * jnp.exp(x)).astype(BF16), s) for qq, x, s in zip(q, b, st_in)]

    a_st = [None] * nch
    for m in levels:
        br = [jnp.concatenate(
            [jnp.broadcast_to(x[g0 + m:g0 + m + 1, :], (2 * m, w)) for g0 in range(0, c, 2 * m)], axis=0)
            for x in b]
        e = [jnp.exp(-jnp.abs(x - r)) for x, r in zip(b, br)]
        second = second_half[m]
        qe = [jnp.where(second, qq * ee, 0.0).astype(BF16) for qq, ee in zip(q, e)]
        ke = [jnp.where(second, 0.0, kk * ee).astype(BF16) for kk, ee in zip(k_in, e)]
        q_stk = [jnp.where(head_rows, jnp.concatenate([x] * HG_HEADS, axis=0), jnp.zeros((), BF16)) for x in qe]
        am = [_dot_nt(qs, kk) for qs, kk in zip(q_stk, ke)]
        if 2 * m < c:
            am = [jnp.where(same_block[m], x, 0.0) for x in am]
        a_st = [x if a is None else a + x for a, x in zip(a_st, am)]
    o_stk = [_dot(a.astype(BF16), vb) for a, vb in zip(a_st, v_bf)]
    for h in range(HG_HEADS):
        o = [oo + jnp.where(lane_head == h, os[h * c:(h + 1) * c], 0.0) for oo, os in zip(o, o_stk)]

    nt8 = c // SUBLANES
    sub = _iota((SUBLANES, 1), 0)
    ws = [[] for _ in range(nch)]
    for g in range(nt8):
        t0 = g * SUBLANES
        for i in range(SUBLANES):
            ok = sub >= i
            for ci in range(nch):
                qt, bt = q[ci][t0:t0 + SUBLANES, :], b[ci][t0:t0 + SUBLANES, :]
                diff = bt - b[ci][t0 + i:t0 + i + 1, :]
                ex = jnp.exp(diff if i == 0 else jnp.where(ok, diff, NEG_BIG))
                ws[ci].append((qt * ex) * k_in[ci][t0 + i:t0 + i + 1, :])
    cw = [_dot(jnp.concatenate(x, axis=0).astype(BF16), bd) for x in ws]
    for ci in range(nch):
        o_tiles = []
        for g in range(nt8):
            acc = None
            for i in range(SUBLANES):
                s = g * SUBLANES + i
                t = cw[ci][s * SUBLANES:(s + 1) * SUBLANES] * v[ci][s:s + 1, :]
                acc = t if acc is None else acc + t
            o_tiles.append(acc)
        o[ci] = o[ci] + jnp.concatenate(o_tiles, axis=0)
    ms = [_dot_exact_rhs(x * x, bd, 2) * (1.0 / HG_DV) for x in o]
    for ci in range(nch):
        y = (o[ci] * lax.rsqrt(ms[ci] + EPS)) * gain_ref[...]
        o_ref[rows_of[ci], :] = (y * _silu(hg_ref[rows_of[ci], 3 * w:4 * w])).astype(BF16)


def _hgrn2(hg, lb_logits, gain, *, bsz, seq, layer):
    tt = HG_TT
    nt = seq // tt
    w = HG_HEADS * HG_DV
    gain4 = jnp.tile(gain.reshape(1, HG_DV), (1, HG_HEADS))
    return pl.pallas_call(
        functools.partial(_hg_kernel, layer=layer),
        grid=(bsz, nt),
        in_specs=[
            pl.BlockSpec((tt, hg.shape[1]), lambda b, t: (b * nt + t, 0)),
            pl.BlockSpec(lb_logits.shape, lambda b, t: (0, 0)),
            pl.BlockSpec((1, w), lambda b, t: (0, 0)),
        ],
        out_specs=pl.BlockSpec((tt, w), lambda b, t: (b * nt + t, 0)),
        out_shape=jax.ShapeDtypeStruct((bsz * seq, w), BF16),
        scratch_shapes=[pltpu.VMEM((w, w), F32)],
        compiler_params=_params(("arbitrary", "arbitrary")),
        name="hgrn2",
    )(hg, lb_logits, gain4)


def kernel(x, c, ffn1_norm, ffn1_w_in, ffn1_w_out, mix_norm, mix_w_in, gdn_conv_w, gdn_A_log, gdn_dt_bias,
           sb_out_norm, gdn_out_norm, hg_out_norm, hg_lb_logits, mix_w_out, ffn2_norm, ffn2_w_in, ffn2_w_out,
           ada_w, ada_b, final_norm):
    bsz, seq, d = x.shape
    depth = ada_w.shape[0]
    sb_w = SB_HEADS * SB_DIM
    gd_wk = GDN_HEADS * GDN_DK
    gd_wv = GDN_HEADS * GDN_DV
    hg_w = HG_HEADS * HG_DK

    mod = _ada(c, ada_w, ada_b)
    xf = x.reshape(bsz * seq, d)
    for l in range(depth):
        mod_l = mod[l].reshape(bsz, N_MOD, d)
        xf, hf = _ffn(xf, mod_l, ffn1_norm[l], ffn1_w_in[l], ffn1_w_out[l],
                      mix_norm[l], seq=seq, base=0, mode="first")

        wi = mix_w_in[l].astype(BF16)
        o_gd = 3 * sb_w
        o_b = o_gd + 3 * gd_wk + gd_wv
        o_hg = o_b + 2 * GDN_HEADS
        w_ba = jnp.pad(wi[:, o_b:o_hg], ((0, 0), (0, LANES - 2 * GDN_HEADS)))
        sb, gd, hg, ba = _inproj(hf, wi[:, :o_gd], wi[:, o_gd:o_b], wi[:, o_hg:], w_ba,
                                 gdn_conv_w[l], gdn_A_log[l], gdn_dt_bias[l], seq=seq)

        o_sb = _sb_attention(sb, sb_out_norm[l], bsz=bsz, seq=seq)
        o_gdn = _gdn(gd, ba, gdn_out_norm[l], bsz=bsz, seq=seq)
        o_hgr = _hgrn2(hg, hg_lb_logits, hg_out_norm[l], bsz=bsz, seq=seq, layer=l)

        wo = mix_w_out[l].astype(BF16)
        mixer = (o_sb, o_gdn, o_hgr, wo[:sb_w], wo[sb_w:sb_w + gd_wv], wo[sb_w + gd_wv:])
        xf = _ffn(xf, mod_l, ffn2_norm[l], ffn2_w_in[l], ffn2_w_out[l],
                  final_norm, mixer, seq=seq, base=6, mode="last" if l == depth - 1 else "second")
    return xf.reshape(bsz, seq, d)
```

```python
import functools

import jax
import jax.numpy as jnp
from jax import lax
from jax.experimental import pallas as pl
from jax.experimental.pallas import tpu as pltpu

F32 = jnp.float32
BF16 = jnp.bfloat16

EPS = 1e-6
HALF_STEP = 0.5
N_MOD = 9

SB_HEADS, SB_DIM = 4, 64
LOG2E = 1.4426950408889634
SB_Q_SCALE = SB_DIM ** -0.5 * LOG2E
GDN_HEADS, GDN_DK, GDN_DV, GDN_CONV, GDN_CHUNK = 4, 128, 128, 4, 64
HG_HEADS, HG_DK, HG_DV, HG_CHUNK = 4, 64, 64, 64

LANES = 128
SUBLANES = 8
VMEM_LIMIT = 56 * 1024 * 1024

_NT = (((1,), (1,)), ((), ()))
_TN = (((0,), (0,)), ((), ()))


def _dot(a, b):
    return jnp.dot(a, b, preferred_element_type=F32)


def _dot_nt(a, b):
    return lax.dot_general(a, b, _NT, preferred_element_type=F32)


def _dot_tn(a, b):
    return lax.dot_general(a, b, _TN, preferred_element_type=F32)


def _split(x, parts):
    out = []
    r = x
    for _ in range(parts):
        p = r.astype(BF16)
        out.append(p)
        r = r - p.astype(F32)
    return out


def _dot_exact_rhs(x, m, parts):
    acc = None
    for p in _split(x, parts):
        t = _dot(p, m)
        acc = t if acc is None else acc + t
    return acc


def _dot_exact_lhs(m, x, parts):
    acc = None
    for p in _split(x, parts):
        t = _dot(m, p)
        acc = t if acc is None else acc + t
    return acc


def _silu(x):
    return x * jax.nn.sigmoid(x)


def _softplus(x):
    return jnp.maximum(x, 0.0) + jnp.log1p(jnp.exp(-jnp.abs(x)))


def _iota(shape, dim):
    return lax.broadcasted_iota(jnp.int32, shape, dim)


def _params(sem):
    return pltpu.CompilerParams(dimension_semantics=sem, vmem_limit_bytes=VMEM_LIMIT)


def _ada_kernel(c_ref, w_ref, b_ref, o_ref):
    ca = _silu(c_ref[...]).astype(BF16)
    o_ref[0] = _dot(ca, w_ref[0].astype(BF16)) + b_ref[0]


def _ada(c, ada_w, ada_b):
    depth, d, n = ada_w.shape
    bsz = c.shape[0]
    tn = 1152
    assert n % tn == 0
    return pl.pallas_call(
        _ada_kernel,
        grid=(depth, n // tn),
        in_specs=[
            pl.BlockSpec((bsz, d), lambda l, j: (0, 0)),
            pl.BlockSpec((1, d, tn), lambda l, j: (l, 0, j)),
            pl.BlockSpec((1, 1, tn), lambda l, j: (l, 0, j)),
        ],
        out_specs=pl.BlockSpec((1, bsz, tn), lambda l, j: (l, 0, j)),
        out_shape=jax.ShapeDtypeStruct((depth, bsz, n), F32),
        compiler_params=_params(("arbitrary", "arbitrary")),
        name="ada_mod",
    )(c, ada_w, ada_b.reshape(depth, 1, n))


def _norm_mod(x, gain, shift, scale):
    y = x * lax.rsqrt(jnp.mean(x * x, axis=-1, keepdims=True) + EPS)
    return (y * gain) * (1.0 + scale) + shift


MXU_WIDTH = 256


def _ff_chunks(d_ff, target):
    assert d_ff % MXU_WIDTH == 0 and target % MXU_WIDTH == 0
    out, c0 = [], 0
    while c0 < d_ff:
        cn = min(target, d_ff - c0)
        out.append((c0, cn))
        c0 += cn
    return tuple(out)


def _ffn_kernel(*refs, base, mode, chunks):
    x_ref, mod_ref, g_ref, wi_ref, wo_ref, g2_ref = refs[:6]
    wi_sc, wo_sc = refs[-2:]
    refs = refs[:-2]
    d_ff = wo_sc.shape[0]
    step = pl.program_id(0)
    cw, cr = wi_ref.shape[1], wo_ref.shape[0]

    for j in range(W_CAST_STEPS):
        @pl.when(step == j)
        def _(j=j):
            wi_sc[:, j * cw:(j + 1) * cw] = wi_ref[...].astype(BF16)
            wo_sc[j * cr:(j + 1) * cr, :] = wo_ref[...].astype(BF16)

    @pl.when(step >= W_CAST_STEPS)
    def _():
        m = mod_ref[0]
        x = x_ref[...]
        if mode != "first":
            osb_ref, ogd_ref, ohg_ref, wsb_ref, wgd_ref, whg_ref, o_ref = refs[6:]
            y = (_dot(osb_ref[...], wsb_ref[...]) + _dot(ogd_ref[...], wgd_ref[...])
                 + _dot(ohg_ref[...], whg_ref[...]))
            x = x + m[base - 1:base] * y
        else:
            o_ref, h_ref = refs[6:]
        h = _norm_mod(x, g_ref[...], m[base:base + 1], m[base + 1:base + 2]).astype(BF16)
        acc = None
        for c0, cn in chunks:
            gate = _dot(h, wi_sc[:, c0:c0 + cn])
            up = _dot(h, wi_sc[:, d_ff + c0:d_ff + c0 + cn])
            t = _dot((_silu(gate) * up).astype(BF16), wo_sc[c0:c0 + cn, :])
            acc = t if acc is None else acc + t
        out = x + (HALF_STEP * m[base + 2:base + 3]) * acc
        if mode == "last":
            out = (out * lax.rsqrt(jnp.mean(out * out, axis=-1, keepdims=True) + EPS)) * g2_ref[...]
        o_ref[...] = out
        if mode == "first":
            h_ref[...] = _norm_mod(out, g2_ref[...], m[base + 3:base + 4], m[base + 4:base + 5]).astype(BF16)


W_CAST_STEPS = 11


def _ffn(xf, mod_l, gain, w_in, w_out, gain2, mixer=None, *, layer, seq, base, mode):
    bt, d = xf.shape
    d_ff = w_out.shape[1]
    tm = 512
    nc = W_CAST_STEPS
    assert bt % tm == 0 and seq % tm == 0
    assert (2 * d_ff) % (nc * LANES) == 0 and d_ff % (nc * 2 * SUBLANES) == 0
    cw, cr = 2 * d_ff // nc, d_ff // nc
    tile = lambda i: jnp.maximum(i - nc, 0)
    slab = lambda i: jnp.minimum(i, nc - 1)
    row = lambda n: pl.BlockSpec((tm, n), lambda i: (tile(i), 0))
    resident = lambda a: pl.BlockSpec(a.shape, lambda i: (0, 0), pipeline_mode=pl.Buffered(1))
    in_specs = [
        row(d),
        pl.BlockSpec((1, N_MOD, d), lambda i: ((tile(i) * tm) // seq, 0, 0)),
        pl.BlockSpec((1, d), lambda i: (0, 0)),
        pl.BlockSpec((None, d, cw), lambda i: (layer, 0, slab(i))),
        pl.BlockSpec((None, cr, d), lambda i: (layer, slab(i), 0)),
        pl.BlockSpec((1, d), lambda i: (0, 0)),
    ]
    args = [xf, mod_l, gain.reshape(1, d), w_in, w_out, gain2.reshape(1, d)]
    out_specs = row(d)
    out_shape = jax.ShapeDtypeStruct((bt, d), F32)
    if mode == "first":
        out_specs = [row(d), row(d)]
        out_shape = [out_shape, jax.ShapeDtypeStruct((bt, d), BF16)]
    else:
        in_specs += [row(a.shape[1]) for a in mixer[:3]] + [resident(a) for a in mixer[3:]]
        args += list(mixer)
    return pl.pallas_call(
        functools.partial(_ffn_kernel, base=base, mode=mode, chunks=_ff_chunks(d_ff, 1024)),
        grid=(nc + bt // tm,),
        in_specs=in_specs,
        out_specs=out_specs,
        out_shape=out_shape,
        scratch_shapes=[pltpu.VMEM((d, 2 * d_ff), BF16), pltpu.VMEM((d_ff, d), BF16)],
        compiler_params=_params(("arbitrary",)),
        name="ffn_half_step_" + mode,
    )(*args)


HALO = SUBLANES


def _inproj_kernel(h_ref, wsb_ref, wgd_ref, whg_ref, wba_ref, cw_ref, par_ref,
                   sb_ref, gd_ref, hg_ref, ba_ref, ext_sc, *, tiles_per_seq):
    tm = h_ref.shape[0]
    wq = GDN_HEADS * GDN_DK
    nqkv = 3 * wq

    @pl.when(pl.program_id(0) % tiles_per_seq == 0)
    def _():
        ext_sc[0:HALO, :] = jnp.zeros((HALO, nqkv), F32)

    h = h_ref[...]

    def conv_part(part, g):
        c0 = part * wq
        ext_sc[HALO:HALO + tm, c0:c0 + wq] = g
        for hd in range(GDN_HEADS):
            cols = slice(c0 + hd * GDN_DK, c0 + (hd + 1) * GDN_DK)
            y = cw_ref[GDN_CONV - 1:GDN_CONV, cols] * g[:, hd * GDN_DK:(hd + 1) * GDN_DK]
            for i in range(1, GDN_CONV):
                y = y + cw_ref[GDN_CONV - 1 - i:GDN_CONV - i, cols] * ext_sc[HALO - i:HALO - i + tm, cols]
            y = _silu(y)
            if part < 2:
                y = y * lax.rsqrt(jnp.sum(y * y, axis=-1, keepdims=True) + EPS)
                if part == 0:
                    y = y * (GDN_DK ** -0.5)
            gd_ref[:, cols] = y
        ext_sc[0:HALO, c0:c0 + wq] = ext_sc[tm:tm + HALO, c0:c0 + wq]

    nq = SB_HEADS * SB_DIM
    g_q = _dot(h, wgd_ref[:, 0:wq])
    g_k = _dot(h, wgd_ref[:, wq:2 * wq])
    conv_part(0, g_q)
    g_v = _dot(h, wgd_ref[:, 2 * wq:nqkv])
    conv_part(1, g_k)
    gd_ref[:, nqkv:] = _dot(h, wgd_ref[:, nqkv:])
    sb = _dot(h, wsb_ref[...])
    conv_part(2, g_v)
    sb = jnp.where(_iota(sb.shape, 1) < nq, sb * SB_Q_SCALE, sb)
    sb_ref[...] = sb.astype(BF16)
    hg_ref[...] = _dot(h, whg_ref[...])

    ba = _dot(h, wba_ref[...])
    log_a = -jnp.exp(par_ref[0:1, :]) * _softplus(ba + par_ref[1:2, :])
    ba_ref[...] = jnp.where(_iota(ba.shape, 1) < GDN_HEADS, jax.nn.sigmoid(ba), log_a)


def _inproj(hf, w_sb, w_gd, w_hg, w_ba, conv_w, a_log, dt_bias, *, seq):
    bt, d = hf.shape
    tm = 512
    assert bt % tm == 0 and seq % tm == 0
    n_sb, n_gd, n_hg, n_ba = w_sb.shape[1], w_gd.shape[1], w_hg.shape[1], w_ba.shape[1]
    par = jnp.zeros((SUBLANES, LANES), F32)
    par = par.at[0, GDN_HEADS:2 * GDN_HEADS].set(a_log).at[1, GDN_HEADS:2 * GDN_HEADS].set(dt_bias)
    row = lambda n: pl.BlockSpec((tm, n), lambda i: (i, 0))
    full = lambda a: pl.BlockSpec(a.shape, lambda i: (0, 0))
    return pl.pallas_call(
        functools.partial(_inproj_kernel, tiles_per_seq=seq // tm),
        grid=(bt // tm,),
        in_specs=[
            row(d),
            full(w_sb), full(w_gd), full(w_hg), full(w_ba), full(conv_w), full(par),
        ],
        out_specs=[row(n_sb), row(n_gd), row(n_hg), row(n_ba)],
        out_shape=[
            jax.ShapeDtypeStruct((bt, n_sb), BF16),
            jax.ShapeDtypeStruct((bt, n_gd), F32),
            jax.ShapeDtypeStruct((bt, n_hg), F32),
            jax.ShapeDtypeStruct((bt, n_ba), F32),
        ],
        scratch_shapes=[pltpu.VMEM((tm + HALO, 3 * GDN_HEADS * GDN_DK), F32)],
        compiler_params=_params(("arbitrary",)),
        name="mixer_in_proj",
    )(hf, w_sb, w_gd, w_hg, w_ba, conv_w, par)


SB_TQ = 512
SB_TC = 256


def _sb_kernel(q_ref, k_ref, v_ref, gain_ref, o_ref, *, seq):
    tq, tc = SB_TQ, SB_TC
    lane = _iota((1, LANES), 1)
    u = jnp.where(_iota((tc, tc), 0) >= _iota((tc, tc), 1), 1.0, 0.0).astype(BF16)
    hh = _iota((LANES, LANES), 0) // SB_DIM
    hc = _iota((LANES, LANES), 1) // SB_DIM
    bd = jnp.where(hh == hc, 1.0, 0.0).astype(BF16)
    causal = _iota((tc, tc), 1) < _iota((tc, tc), 0)

    def logits(qs, k0s, nk, mask):
        ks = [k_ref[pl.ds(k0, nk), :] for k0 in k0s]
        zs = [_dot_nt(q, k) for q, k in zip(qs, ks)]
        sps = [jnp.maximum(z, 0.0) + jnp.log2(1.0 + jnp.exp2(-jnp.abs(z))) for z in zs]
        if mask is not None:
            sps = [jnp.where(mask, sp, 0.0) for sp in sps]
        return zs, sps

    def accumulate(zs, sps, k0s, nk, carries, accs, mask):
        vs = [v_ref[pl.ds(k0, nk), :] for k0 in k0s]
        ws = [[None] * (nk // tc) for _ in zs]
        for g in reversed(range(nk // tc)):
            cols = slice(g * tc, (g + 1) * tc)
            rs = [_dot(sp[:, cols].astype(BF16), u) + c for sp, c in zip(sps, carries)]
            for i, (z, r) in enumerate(zip(zs, rs)):
                ws[i][g] = jnp.exp2(z[:, cols] - r)
            carries = [c + jnp.sum(sp[:, cols], axis=-1, keepdims=True) for sp, c in zip(sps, carries)]
        ws = [w[0] if len(w) == 1 else jnp.concatenate(w, axis=1) for w in ws]
        if mask is not None:
            ws = [jnp.where(mask, w, 0.0) for w in ws]
        accs = [a + _dot(w.astype(BF16), v) for a, w, v in zip(accs, ws, vs)]
        return carries, accs

    def step(qs, k0s, nk, carries, accs, mask):
        zs, sps = logits(qs, k0s, nk, mask)
        return accumulate(zs, sps, k0s, nk, carries, accs, mask)

    nhp = LANES // SB_DIM
    nparts = tq // tc

    def qblock(qi, _):
        q0 = pl.multiple_of(qi * tq, tq)
        q2 = q_ref[pl.ds(q0, tq), :]
        hms = [(lane // SB_DIM) == h for h in range(nhp)]
        qhs = [jnp.where(hm, q2, jnp.zeros_like(q2)) for hm in hms]
        qps = [qh[p * tc:(p + 1) * tc] for qh in qhs for p in range(nparts)]
        k0s = [pl.multiple_of(q0 + p * tc, tc) for _ in qhs for p in range(nparts)]
        cs = [jnp.zeros((tc, 1), F32) for _ in qps]
        accs = [jnp.zeros((tc, LANES), F32) for _ in qps]
        cs, accs = step(qps, k0s, tc, cs, accs, causal)
        for back in range(1, nparts):
            sel = [i for i in range(len(qps)) if i % nparts >= back]
            c2, a2 = step([qps[i] for i in sel],
                          [pl.multiple_of(q0 + (i % nparts - back) * tc, tc) for i in sel], tc,
                          [cs[i] for i in sel], [accs[i] for i in sel], None)
            for j, i in enumerate(sel):
                cs[i], accs[i] = c2[j], a2[j]
        carries = [jnp.concatenate(cs[h * nparts:(h + 1) * nparts], axis=0) for h in range(nhp)]
        accs = [jnp.concatenate(accs[h * nparts:(h + 1) * nparts], axis=0) for h in range(nhp)]

        out = jnp.zeros((tq, LANES), F32)
        for h in range(nhp):
            def body(i, ca, qh=qhs[h]):
                k0 = pl.multiple_of((qi - 1 - i) * tq, tq)
                c2, a2 = step([qh], [k0], tq, [ca[0]], [ca[1]], None)
                return c2[0], a2[0]

            _, acc = lax.fori_loop(0, qi, body, (carries[h], accs[h]))
            out = out + jnp.where(hms[h], acc, 0.0)
        ms = _dot_exact_rhs(out * out, bd, 2) * (1.0 / SB_DIM)
        y = (out * lax.rsqrt(ms + EPS)) * gain_ref[...]
        o_ref[pl.ds(q0, tq), :] = y.astype(BF16)
        return 0

    lax.fori_loop(0, seq // tq, qblock, 0)


def _sb_attention(sb, gain, *, bsz, seq):
    npair = SB_HEADS * SB_DIM // LANES
    gain2 = jnp.tile(gain.reshape(1, SB_DIM), (1, LANES // SB_DIM))
    blk = lambda off: pl.BlockSpec((seq, LANES), lambda b, p: (b, off + p))
    return pl.pallas_call(
        functools.partial(_sb_kernel, seq=seq),
        grid=(bsz, npair),
        in_specs=[blk(0), blk(npair), blk(2 * npair), pl.BlockSpec((1, LANES), lambda b, p: (0, 0))],
        out_specs=pl.BlockSpec((seq, LANES), lambda b, p: (b, p)),
        out_shape=jax.ShapeDtypeStruct((bsz * seq, npair * LANES), BF16),
        compiler_params=_params(("arbitrary", "arbitrary")),
        name="sb_attention",
    )(sb, sb, sb, gain2)


GDN_TT = 512


def _gdn_kernel(gd_ref, ba_ref, gain_ref, o_ref, bl_sc, s_sc):
    tt = GDN_TT
    c = GDN_CHUNK
    nh = GDN_HEADS
    dk = GDN_DK
    wq = nh * dk
    nqkv = 3 * wq
    hpu = LANES // c
    hc = hpu * c
    ngr = nh // hpu
    assert hc == LANES and nh % hpu == 0
    qkv_sc = gd_ref

    @pl.when(pl.program_id(1) == 0)
    def _():
        s_sc[...] = jnp.zeros_like(s_sc)

    ba = ba_ref[...]
    lane = _iota((1, LANES), 1)
    for j in range(2 * nh):
        col = jnp.sum(jnp.where(lane == j, ba, 0.0), axis=-1, keepdims=True)
        bl_sc[j] = jnp.broadcast_to(col, (tt, LANES))

    r = _iota((hc, hc), 0)
    s = _iota((hc, hc), 1)
    same_head = (r // c) == (s // c)
    incl = same_head & (s <= r)
    strict = same_head & (s < r)
    tri = jnp.where(incl, 1.0, 0.0).astype(BF16)
    eye = jnp.where(r == s, 1.0, 0.0)

    rxs = r ^ s
    lmask = {m: jnp.where((s < r) & (rxs >= m) & (rxs < 2 * m), 1.0, 0.0).astype(BF16)
             for m in (1, 2, 4, 8, 16, 32)}
    assert 2 * max(lmask) == c

    nch = tt // c
    rows_of = [pl.ds(ci * c, c) for ci in range(nch)]
    units = [(ci, gi) for ci in range(nch) for gi in range(ngr)]
    heads_of = lambda gi: range(gi * hpu, (gi + 1) * hpu)

    def stack(unit, off):
        ci, gi = unit
        return jnp.concatenate(
            [qkv_sc[rows_of[ci], off + h * dk:off + (h + 1) * dk] for h in heads_of(gi)], axis=0)

    def stack_col(unit, j0):
        ci, gi = unit
        return jnp.concatenate([bl_sc[j0 + h, rows_of[ci], :] for h in heads_of(gi)], axis=0)

    q_st = [stack(un, 0) for un in units]
    k_st = [stack(un, wq) for un in units]
    v_st = [stack(un, 2 * wq) for un in units]
    beta_st = [stack_col(un, 0) for un in units]
    la_bc = [stack_col(un, nh) for un in units]
    g_bc = [_dot_exact_lhs(tri, la, 3) for la in la_bc]
    g_last = [jnp.concatenate(
        [jnp.broadcast_to(g[j * c + c - 1:j * c + c, :], (c, LANES)) for j in range(hpu)], axis=0)
        for g in g_bc]
    ex = [jnp.exp(g - jnp.broadcast_to(g.T[0:1, :], (hc, hc))) for g in g_bc]
    kb_st = [k * b for k, b in zip(k_st, beta_st)]
    k_bf = [k.astype(BF16) for k in k_st]
    a_kk = [(_dot_nt(kb.astype(BF16), kf) * jnp.where(strict, e, 0.0)).astype(BF16)
            for kb, kf, e in zip(kb_st, k_bf, ex)]
    a_qk = [(_dot_nt(q.astype(BF16), kf) * jnp.where(incl, e, 0.0)).astype(BF16)
            for q, kf, e in zip(q_st, k_bf, ex)]

    xs = [eye - (a * lmask[1]).astype(F32) for a in a_kk]
    m = 2
    while m < c:
        xb = [x.astype(BF16) for x in xs]
        ys = [_dot(a * lmask[m], b).astype(BF16) for a, b in zip(a_kk, xb)]
        xs = [x - _dot(b, y) for x, b, y in zip(xs, xb, ys)]
        m *= 2

    eg = [jnp.exp(g) for g in g_bc]
    uw = [_dot(x.astype(BF16), jnp.concatenate([v * b, kb * e], axis=1).astype(BF16))
          for x, v, b, kb, e in zip(xs, v_st, beta_st, kb_st, eg)]
    q_decs = [(q * e).astype(BF16) for q, e in zip(q_st, eg)]
    k_decs = [(k * jnp.exp(gl - g)).astype(BF16) for k, gl, g in zip(k_st, g_last, g_bc)]
    cds = [jnp.exp(gl) for gl in g_last]

    hs = [slice(j * c, (j + 1) * c) for j in range(hpu)]
    u_bf = [x[:, :dk].astype(BF16) for x in uw]
    w_bf = [x[:, dk:].astype(BF16) for x in uw]
    unit_of = lambda ci, h: (ci * ngr + h // hpu, hs[h % hpu])
    assert nch % 2 == 0
    amap = [[None] * nh for _ in range(nch)]
    for ci in range(nch):
        for h in range(nh):
            un, sl = unit_of(ci, h)
            n_f = _dot_tn(k_decs[un][sl], w_bf[un][sl])
            amap[ci][h] = (cds[un][sl][0:1, :], n_f, n_f.astype(BF16), _dot_tn(k_decs[un][sl], u_bf[un][sl]))
    pair = [[None] * nh for _ in range(nch // 2)]
    for p in range(nch // 2):
        for h in range(nh):
            (a0, n0, n0b, q0), (a1, n1, n1b, q1) = amap[2 * p][h], amap[2 * p + 1][h]
            n01 = a1 * n0 + a0 * n1 - _dot(n1b, n0b)
            pair[p][h] = (a0 * a1, n01.astype(BF16), a1 * q0 + q1 - _dot(n1b, q0.astype(BF16)))
    s_cur = [s_sc[h] for h in range(nh)]
    s_in = [None] * nch
    for p in range(nch // 2):
        s_bf = [s.astype(BF16) for s in s_cur]
        s_in[2 * p] = s_bf
        s_in[2 * p + 1] = [(s * amap[2 * p][h][0] + amap[2 * p][h][3] - _dot(amap[2 * p][h][2], s_bf[h])
                            ).astype(BF16) for h, s in enumerate(s_cur)]
        s_cur = [s * pair[p][h][0] + pair[p][h][2] - _dot(pair[p][h][1], s_bf[h])
                 for h, s in enumerate(s_cur)]
    for h in range(nh):
        s_sc[h] = s_cur[h]

    v_new = [jnp.concatenate(
        [uw[un][sl, :dk] - _dot(w_bf[un][sl], s_in[ci][h]) for h, sl in zip(heads_of(gi), hs)], axis=0)
        for un, (ci, gi) in enumerate(units)]
    o_un = [jnp.concatenate([_dot(q_decs[un][sl], s_in[ci][h]) for h, sl in zip(heads_of(gi), hs)], axis=0)
            for un, (ci, gi) in enumerate(units)]
    o_un = [o + _dot(a, vn.astype(BF16)) for o, a, vn in zip(o_un, a_qk, v_new)]
    ms = [jnp.mean(o * o, axis=-1, keepdims=True) for o in o_un]
    o_un = [(o * lax.rsqrt(m_ + EPS)) * gain_ref[...] for o, m_ in zip(o_un, ms)]
    for ci in range(nch):
        rows = rows_of[ci]
        o_t = jnp.concatenate(
            [o_un[ci * ngr + gi][sl] for gi in range(ngr) for sl in hs], axis=1)
        z = gd_ref[rows, nqkv:nqkv + wq]
        o_ref[rows, :] = (o_t * _silu(z)).astype(BF16)


def _gdn(gd, ba, gain, *, bsz, seq):
    tt = GDN_TT
    nt = seq // tt
    wq = GDN_HEADS * GDN_DK
    return pl.pallas_call(
        _gdn_kernel,
        grid=(bsz, nt),
        in_specs=[
            pl.BlockSpec((tt, gd.shape[1]), lambda b, t: (b * nt + t, 0)),
            pl.BlockSpec((tt, LANES), lambda b, t: (b * nt + t, 0)),
            pl.BlockSpec((1, GDN_DV), lambda b, t: (0, 0)),
        ],
        out_specs=pl.BlockSpec((tt, wq), lambda b, t: (b * nt + t, 0)),
        out_shape=jax.ShapeDtypeStruct((bsz * seq, wq), BF16),
        scratch_shapes=[
            pltpu.VMEM((2 * GDN_HEADS, tt, LANES), F32),
            pltpu.VMEM((GDN_HEADS, GDN_DK, GDN_DV), F32),
        ],
        compiler_params=_params(("arbitrary", "arbitrary")),
        name="gated_deltanet",
    )(gd, ba, gain.reshape(1, GDN_DV))


HG_TT = 256
NEG_BIG = -1e30


def _hg_kernel(hg_ref, lbl_ref, gain_ref, o_ref, st_sc, *, layer):
    tt = HG_TT
    c = HG_CHUNK
    w = HG_HEADS * HG_DK
    ti = pl.program_id(1)

    @pl.when(ti == 0)
    def _():
        st_sc[...] = jnp.zeros_like(st_sc)

    lg = lbl_ref[...]
    e = jnp.exp(lg - jnp.max(lg, axis=0, keepdims=True))
    p = e / jnp.sum(e, axis=0, keepdims=True)
    lb = jnp.zeros((1, w), F32)
    for i in range(1, layer + 1):
        lb = lb + p[i:i + 1, :]
    log_lb = jnp.log(lb)
    log1m_lb = jnp.log1p(-lb)

    hr = _iota((w, w), 0) // HG_DK
    hcn = _iota((w, w), 1) // HG_DK
    same_head = hr == hcn
    bd = jnp.where(same_head, 1.0, 0.0).astype(BF16)
    tr = _iota((c, c), 0)
    ts = _iota((c, c), 1)
    tri = jnp.where(ts <= tr, 1.0, 0.0).astype(BF16)
    levels = (SUBLANES, 2 * SUBLANES, 4 * SUBLANES)
    assert 2 * levels[-1] == c
    trow = _iota((c, w), 0)
    second_half = {m: ((trow // m) % 2) == 1 for m in levels}
    lane_head = _iota((1, w), 1) // HG_DK
    head_rows = (_iota((HG_HEADS * c, w), 0) // c) == (_iota((HG_HEADS * c, w), 1) // HG_DK)
    st_t = _iota((HG_HEADS * c, c), 0) % c
    st_s = _iota((HG_HEADS * c, c), 1)
    same_block = {m: (st_t // (2 * m)) == (st_s // (2 * m)) for m in levels}

    nch = tt // c
    rows_of = [pl.ds(ci * c, c) for ci in range(nch)]
    q = [_silu(hg_ref[rw, 0:w]) for rw in rows_of]
    f_pre = [hg_ref[rw, w:2 * w] for rw in rows_of]
    v = [hg_ref[rw, 2 * w:3 * w] for rw in rows_of]
    v_bf = [x.astype(BF16) for x in v]
    log_sig = [jnp.minimum(f, 0.0) - jnp.log(1.0 + jnp.exp(-jnp.abs(f))) for f in f_pre]
    bterm = [log1m_lb + x for x in log_sig]
    lf = [jnp.maximum(log_lb, x) + jnp.log(1.0 + jnp.exp(-jnp.abs(log_lb - x))) for x in bterm]
    log_k = [x - f for x, f in zip(bterm, f_pre)]
    k_in = [jnp.exp(x) for x in log_k]
    b = [_dot_exact_lhs(tri, x, 3) for x in lf]
    b_last = [x[c - 1:c, :] for x in b]

    k_dec = [(k * jnp.exp(bl - x)).astype(BF16) for k, bl, x in zip(k_in, b_last, b)]
    upd = [jnp.where(same_head, _dot_tn(vb, kd), 0.0) for vb, kd in zip(v_bf, k_dec)]
    dec = [jnp.exp(bl) for bl in b_last]
    st = st_sc[...]
    st_in = []
    for ci in range(nch):
        st_in.append(st.astype(BF16))
        st = st * dec[ci] + upd[ci]
    st_sc[...] = st
    o = [_dot_nt((qq * jnp.exp(x)).astype(BF16), s) for qq, x, s in zip(q, b, st_in)]

    a_st = [None] * nch
    for m in levels:
        br = [jnp.concatenate(
            [jnp.broadcast_to(x[g0 + m:g0 + m + 1, :], (2 * m, w)) for g0 in range(0, c, 2 * m)], axis=0)
            for x in b]
        e = [jnp.exp(-jnp.abs(x - r)) for x, r in zip(b, br)]
        second = second_half[m]
        qe = [jnp.where(second, qq * ee, 0.0).astype(BF16) for qq, ee in zip(q, e)]
        ke = [jnp.where(second, 0.0, kk * ee).astype(BF16) for kk, ee in zip(k_in, e)]
        q_stk = [jnp.where(head_rows, jnp.concatenate([x] * HG_HEADS, axis=0), jnp.zeros((), BF16)) for x in qe]
        am = [_dot_nt(qs, kk) for qs, kk in zip(q_stk, ke)]
        if 2 * m < c:
            am = [jnp.where(same_block[m], x, 0.0) for x in am]
        a_st = [x if a is None else a + x for a, x in zip(a_st, am)]
    o_stk = [_dot(a.astype(BF16), vb) for a, vb in zip(a_st, v_bf)]
    for h in range(HG_HEADS):
        o = [oo + jnp.where(lane_head == h, os[h * c:(h + 1) * c], 0.0) for oo, os in zip(o, o_stk)]

    nt8 = c // SUBLANES
    sub = _iota((SUBLANES, 1), 0)
    bk = [x - lk for x, lk in zip(b, log_k)]
    ws = [[] for _ in range(nch)]
    for g in range(nt8):
        t0 = g * SUBLANES
        for i in range(SUBLANES):
            ok = sub >= i
            for ci in range(nch):
                qt, bt = q[ci][t0:t0 + SUBLANES, :], b[ci][t0:t0 + SUBLANES, :]
                diff = bt - bk[ci][t0 + i:t0 + i + 1, :]
                ws[ci].append(qt * jnp.exp(diff if i == 0 else jnp.where(ok, diff, NEG_BIG)))
    cw = [_dot(jnp.concatenate(x, axis=0).astype(BF16), bd) for x in ws]
    for ci in range(nch):
        o_tiles = []
        for g in range(nt8):
            acc = None
            for i in range(SUBLANES):
                s = g * SUBLANES + i
                t = cw[ci][s * SUBLANES:(s + 1) * SUBLANES] * v[ci][s:s + 1, :]
                acc = t if acc is None else acc + t
            o_tiles.append(acc)
        o[ci] = o[ci] + jnp.concatenate(o_tiles, axis=0)
    ms = [_dot_exact_rhs(x * x, bd, 2) * (1.0 / HG_DV) for x in o]
    for ci in range(nch):
        y = (o[ci] * lax.rsqrt(ms[ci] + EPS)) * gain_ref[...]
        o_ref[rows_of[ci], :] = (y * _silu(hg_ref[rows_of[ci], 3 * w:4 * w])).astype(BF16)


def _hgrn2(hg, lb_logits, gain, *, bsz, seq, layer):
    tt = HG_TT
    nt = seq // tt
    w = HG_HEADS * HG_DV
    gain4 = jnp.tile(gain.reshape(1, HG_DV), (1, HG_HEADS))
    return pl.pallas_call(
        functools.partial(_hg_kernel, layer=layer),
        grid=(bsz, nt),
        in_specs=[
            pl.BlockSpec((tt, hg.shape[1]), lambda b, t: (b * nt + t, 0)),
            pl.BlockSpec(lb_logits.shape, lambda b, t: (0, 0)),
            pl.BlockSpec((1, w), lambda b, t: (0, 0)),
        ],
        out_specs=pl.BlockSpec((tt, w), lambda b, t: (b * nt + t, 0)),
        out_shape=jax.ShapeDtypeStruct((bsz * seq, w), BF16),
        scratch_shapes=[pltpu.VMEM((w, w), F32)],
        compiler_params=_params(("arbitrary", "arbitrary")),
        name="hgrn2",
    )(hg, lb_logits, gain4)


def kernel(x, c, ffn1_norm, ffn1_w_in, ffn1_w_out, mix_norm, mix_w_in, gdn_conv_w, gdn_A_log, gdn_dt_bias,
           sb_out_norm, gdn_out_norm, hg_out_norm, hg_lb_logits, mix_w_out, ffn2_norm, ffn2_w_in, ffn2_w_out,
           ada_w, ada_b, final_norm):
    bsz, seq, d = x.shape
    depth = ada_w.shape[0]
    sb_w = SB_HEADS * SB_DIM
    gd_wk = GDN_HEADS * GDN_DK
    gd_wv = GDN_HEADS * GDN_DV
    hg_w = HG_HEADS * HG_DK

    mod = _ada(c, ada_w, ada_b)
    xf = x.reshape(bsz * seq, d)
    for l in range(depth):
        mod_l = mod[l].reshape(bsz, N_MOD, d)
        xf, hf = _ffn(xf, mod_l, ffn1_norm[l], ffn1_w_in, ffn1_w_out,
                      mix_norm[l], layer=l, seq=seq, base=0, mode="first")

        wi = mix_w_in[l].astype(BF16)
        o_gd = 3 * sb_w
        o_b = o_gd + 3 * gd_wk + gd_wv
        o_hg = o_b + 2 * GDN_HEADS
        w_ba = jnp.pad(wi[:, o_b:o_hg], ((0, 0), (0, LANES - 2 * GDN_HEADS)))
        sb, gd, hg, ba = _inproj(hf, wi[:, :o_gd], wi[:, o_gd:o_b], wi[:, o_hg:], w_ba,
                                 gdn_conv_w[l], gdn_A_log[l], gdn_dt_bias[l], seq=seq)

        o_sb = _sb_attention(sb, sb_out_norm[l], bsz=bsz, seq=seq)
        o_gdn = _gdn(gd, ba, gdn_out_norm[l], bsz=bsz, seq=seq)
        o_hgr = _hgrn2(hg, hg_lb_logits, hg_out_norm[l], bsz=bsz, seq=seq, layer=l)

        wo = mix_w_out[l].astype(BF16)
        mixer = (o_sb, o_gdn, o_hgr, wo[:sb_w], wo[sb_w:sb_w + gd_wv], wo[sb_w + gd_wv:])
        xf = _ffn(xf, mod_l, ffn2_norm[l], ffn2_w_in, ffn2_w_out, final_norm, mixer,
                  layer=l, seq=seq, base=6, mode="last" if l == depth - 1 else "second")
    return xf.reshape(bsz, seq, d)
```
